```python
import math
import jax, jax.numpy as jnp
from jax import lax
import numpy as np

D_MODEL = 1024
BATCH = 4
SEQ = 8192
DEPTH = 1

D_MIX = D_MODEL
N_DIFF_HEADS = 8
DIFF_HEAD_DIM = 32
DIFF_V_DIM = 2 * DIFF_HEAD_DIM
D_ATTN = N_DIFF_HEADS * DIFF_V_DIM
D_SSM = D_MIX - D_ATTN
SSM_GROUP = 16
N_SSM_GROUPS = D_SSM // SSM_GROUP
SSM_STATE = 64
D_IN_PROJ = 3 * D_ATTN + D_SSM
N_EXPERTS = 32
TOP_K = 4
D_FF = D_MODEL
SWIGLU_LIMIT = 7.0
SWIGLU_ALPHA = 1.702
PLE_DIM = 256
Q_BLOCK = 128
MOE_BLOCK = 128
LN_EPS = 1e-5
RMS_EPS = 1e-5
DEEPNORM_ALPHA = (2.0 * DEPTH) ** 0.25
DEEPNORM_BETA = (8.0 * DEPTH) ** -0.25

kernel_name = "hymba_diffattn_s5_moe_deepnorm"


def layer_norm(x, g, b):
    xf = x.astype(jnp.float32)
    mu = jnp.mean(xf, axis=-1, keepdims=True)
    var = jnp.mean(jnp.square(xf - mu), axis=-1, keepdims=True)
    return ((xf - mu) * lax.rsqrt(var + LN_EPS) * g + b).astype(x.dtype)


def rms_norm(x, g):
    xf = x.astype(jnp.float32)
    return (xf * lax.rsqrt(jnp.mean(jnp.square(xf), axis=-1, keepdims=True) + RMS_EPS) * g).astype(x.dtype)


def alibi_slopes(n_heads):
    return jnp.exp2(-(jnp.arange(n_heads, dtype=jnp.float32) + 1.0) * (8.0 / n_heads))


def diff_attention(q, k, v, lam, subln_g, lambda_init):
    Bsz, H, L, _ = v.shape
    nb = L // Q_BLOCK
    scale = DIFF_HEAD_DIM ** -0.5
    slopes = alibi_slopes(H)
    kpos = jnp.arange(L, dtype=jnp.int32)
    qb = q.reshape(Bsz, H, nb, Q_BLOCK, 2, DIFF_HEAD_DIM).transpose(2, 0, 1, 3, 4, 5)

    def block(args):
        i, qi = args
        qpos = i * Q_BLOCK + jnp.arange(Q_BLOCK, dtype=jnp.int32)
        s = jnp.einsum('bhqmd,bhkmd->bhmqk', qi, k,
                       preferred_element_type=jnp.float32) * scale
        dist = (qpos[:, None] - kpos[None, :]).astype(jnp.float32)
        bias = jnp.where(dist >= 0, -slopes[:, None, None] * dist, -jnp.inf)
        probs = jax.nn.softmax(s + bias[None, :, None], axis=-1)
        w = probs[:, :, 0] - lam * probs[:, :, 1]
        return jnp.einsum('bhqk,bhkd->bhqd', w.astype(v.dtype), v)

    o = lax.map(block, (jnp.arange(nb, dtype=jnp.int32), qb))
    o = o.transpose(1, 2, 0, 3, 4).reshape(Bsz, H, L, DIFF_V_DIM)
    o = rms_norm(o, subln_g) * (1.0 - lambda_init)
    return o.transpose(0, 2, 1, 3).reshape(Bsz, L, H * DIFF_V_DIM)


def _ssm_combine(e1, e2):
    a1, b1 = e1
    a2, b2 = e2
    return (a2 * a1, a2 * b1 + b2)


def s5_ssm(u, a_re, a_im, log_dt, b_re, b_im, c_re, c_im, d_skip):
    Bsz, L, _ = u.shape
    f32 = jnp.float32
    A = lax.complex(a_re.astype(f32), a_im.astype(f32))
    dt = jnp.exp(log_dt.astype(f32))[:, None]
    A_bar = jnp.exp(A * dt)
    Bm = lax.complex(b_re.astype(f32), b_im.astype(f32))
    B_bar = ((A_bar - 1.0) / A)[..., None] * Bm
    Cm = lax.complex(c_re.astype(f32), c_im.astype(f32))
    uf = u.astype(f32)
    ug = uf.reshape(Bsz, L, N_SSM_GROUPS, SSM_GROUP)

    def one(u_b):
        bu = jnp.einsum('lgc,gpc->lgp', u_b, B_bar)
        a = jnp.broadcast_to(A_bar, bu.shape)
        _, states = lax.associative_scan(_ssm_combine, (a, bu), axis=0)
        return jnp.einsum('lgp,gcp->lgc', states, Cm).real

    y = lax.map(one, ug).reshape(Bsz, L, D_SSM)
    y = y + d_skip.astype(f32) * uf
    return y.astype(u.dtype)


def moe(x, w_router, b_router, w_gate_up, b_gate_up, w_down, b_down):
    Bsz, L, D = x.shape
    T = Bsz * L
    xf = x.reshape(T, D)
    logits = (xf @ w_router + b_router).astype(jnp.float32)
    top_vals, top_idx = lax.top_k(logits, TOP_K)
    gates = jax.nn.softmax(top_vals, axis=-1)
    n_assign = T * TOP_K
    flat_e = top_idx.reshape(-1).astype(jnp.int32)
    flat_tok = jnp.arange(n_assign, dtype=jnp.int32) // TOP_K
    flat_gate = gates.reshape(-1)
    order = jnp.argsort(flat_e)
    sorted_e = flat_e[order]
    counts = jnp.bincount(flat_e, length=N_EXPERTS).astype(jnp.int32)
    padded = ((counts + MOE_BLOCK - 1) // MOE_BLOCK) * MOE_BLOCK
    pad_end = jnp.cumsum(padded)
    pad_start = pad_end - padded
    start = jnp.cumsum(counts) - counts
    rank = jnp.arange(n_assign, dtype=jnp.int32) - start[sorted_e]
    dest = pad_start[sorted_e] + rank
    n_pad = n_assign + N_EXPERTS * MOE_BLOCK
    n_blocks = n_pad // MOE_BLOCK
    slot_tok = jnp.zeros((n_pad,), jnp.int32).at[dest].set(flat_tok[order])
    slot_gate = jnp.zeros((n_pad,), jnp.float32).at[dest].set(flat_gate[order])
    block_e = jnp.minimum(
        jnp.searchsorted(pad_end, jnp.arange(n_blocks, dtype=jnp.int32) * MOE_BLOCK, side='right'),
        N_EXPERTS - 1).astype(jnp.int32)

    def expert_block(args):
        e, tok = args
        xb = xf[tok]
        h = xb @ w_gate_up[e] + b_gate_up[e]
        x_glu = jnp.minimum(h[:, ::2], SWIGLU_LIMIT)
        x_lin = jnp.clip(h[:, 1::2], -SWIGLU_LIMIT, SWIGLU_LIMIT)
        act = x_glu * jax.nn.sigmoid(SWIGLU_ALPHA * x_glu) * (x_lin + 1.0)
        return act @ w_down[e] + b_down[e]

    out = lax.map(expert_block, (block_e, slot_tok.reshape(n_blocks, MOE_BLOCK)))
    out = out.reshape(n_pad, D) * slot_gate[:, None].astype(x.dtype)
    y = jnp.zeros_like(xf).at[slot_tok].add(out)
    return y.reshape(Bsz, L, D)


def setup_inputs(seed: int = 0) -> dict:
    key = jax.random.key(seed)
    ks = jax.random.split(key, 40)
    f32 = jnp.float32

    def nrm(k, shape, s):
        return s * jax.random.normal(k, shape, f32)

    n_idx = jnp.arange(SSM_STATE, dtype=f32)
    return {
        "x": nrm(ks[0], (BATCH, SEQ, D_MODEL), 1.0),
        "p": nrm(ks[1], (DEPTH, BATCH, SEQ, PLE_DIM), 1.0),
        "w_in": nrm(ks[2], (DEPTH, D_MODEL, D_IN_PROJ), D_MODEL ** -0.5),
        "lambda_q1": nrm(ks[3], (DEPTH, DIFF_HEAD_DIM), 0.1),
        "lambda_k1": nrm(ks[4], (DEPTH, DIFF_HEAD_DIM), 0.1),
        "lambda_q2": nrm(ks[5], (DEPTH, DIFF_HEAD_DIM), 0.1),
        "lambda_k2": nrm(ks[6], (DEPTH, DIFF_HEAD_DIM), 0.1),
        "subln_g": 1.0 + nrm(ks[7], (DEPTH, DIFF_V_DIM), 0.02),
        "ssm_a_re": -0.5 + nrm(ks[8], (DEPTH, N_SSM_GROUPS, SSM_STATE), 0.01),
        "ssm_a_im": jnp.pi * n_idx[None, None, :] + nrm(ks[9], (DEPTH, N_SSM_GROUPS, SSM_STATE), 0.01),
        "ssm_log_dt": jax.random.uniform(ks[10], (DEPTH, N_SSM_GROUPS), f32,
                                         math.log(1e-3), math.log(1e-1)),
        "ssm_b_re": nrm(ks[11], (DEPTH, N_SSM_GROUPS, SSM_STATE, SSM_GROUP), (0.5 / SSM_GROUP) ** 0.5),
        "ssm_b_im": nrm(ks[12], (DEPTH, N_SSM_GROUPS, SSM_STATE, SSM_GROUP), (0.5 / SSM_GROUP) ** 0.5),
        "ssm_c_re": nrm(ks[13], (DEPTH, N_SSM_GROUPS, SSM_GROUP, SSM_STATE), (0.5 / SSM_STATE) ** 0.5),
        "ssm_c_im": nrm(ks[14], (DEPTH, N_SSM_GROUPS, SSM_GROUP, SSM_STATE), (0.5 / SSM_STATE) ** 0.5),
        "ssm_d": nrm(ks[15], (DEPTH, D_SSM), 1.0),
        "w_glu": nrm(ks[16], (DEPTH, D_SSM, 2 * D_SSM), D_SSM ** -0.5),
        "ssm_norm_g": 1.0 + nrm(ks[17], (DEPTH, D_SSM), 0.02),
        "w_out": nrm(ks[18], (DEPTH, D_MIX, D_MODEL), D_MIX ** -0.5 * DEEPNORM_BETA),
        "ln1_g": 1.0 + nrm(ks[19], (DEPTH, D_MODEL), 0.02),
        "ln1_b": nrm(ks[20], (DEPTH, D_MODEL), 0.01),
        "w_router": nrm(ks[21], (DEPTH, D_MODEL, N_EXPERTS), D_MODEL ** -0.5),
        "b_router": nrm(ks[22], (DEPTH, N_EXPERTS), 0.01),
        "w_gate_up": nrm(ks[23], (DEPTH, N_EXPERTS, D_MODEL, 2 * D_FF), D_MODEL ** -0.5),
        "b_gate_up": nrm(ks[24], (DEPTH, N_EXPERTS, 2 * D_FF), 0.01),
        "w_down": nrm(ks[25], (DEPTH, N_EXPERTS, D_FF, D_MODEL), D_FF ** -0.5 * DEEPNORM_BETA),
        "b_down": nrm(ks[26], (DEPTH, N_EXPERTS, D_MODEL), 0.01),
        "w_ple_gate": nrm(ks[27], (DEPTH, D_MODEL, D_MODEL), D_MODEL ** -0.5),
        "w_ple_proj": nrm(ks[28], (DEPTH, PLE_DIM, D_MODEL), PLE_DIM ** -0.5 * DEEPNORM_BETA),
        "ln2_g": 1.0 + nrm(ks[29], (DEPTH, D_MODEL), 0.02),
        "ln2_b": nrm(ks[30], (DEPTH, D_MODEL), 0.01),
    }


def reference(x, p, w_in, lambda_q1, lambda_k1, lambda_q2, lambda_k2, subln_g,
              ssm_a_re, ssm_a_im, ssm_log_dt, ssm_b_re, ssm_b_im, ssm_c_re, ssm_c_im,
              ssm_d, w_glu, ssm_norm_g, w_out, ln1_g, ln1_b,
              w_router, b_router, w_gate_up, b_gate_up, w_down, b_down,
              w_ple_gate, w_ple_proj, ln2_g, ln2_b):
    Bsz, L, _ = x.shape
    for i in range(DEPTH):
        lambda_init = 0.8 - 0.6 * math.exp(-0.3 * i)
        h = x @ w_in[i]
        q, k, v, u = jnp.split(h, [D_ATTN, 2 * D_ATTN, 3 * D_ATTN], axis=-1)
        q = q.reshape(Bsz, L, N_DIFF_HEADS, 2, DIFF_HEAD_DIM).transpose(0, 2, 1, 3, 4)
        k = k.reshape(Bsz, L, N_DIFF_HEADS, 2, DIFF_HEAD_DIM).transpose(0, 2, 1, 3, 4)
        v = v.reshape(Bsz, L, N_DIFF_HEADS, DIFF_V_DIM).transpose(0, 2, 1, 3)
        lam = (jnp.exp(jnp.sum(lambda_q1[i].astype(jnp.float32) * lambda_k1[i].astype(jnp.float32)))
               - jnp.exp(jnp.sum(lambda_q2[i].astype(jnp.float32) * lambda_k2[i].astype(jnp.float32)))
               + lambda_init)
        attn_out = diff_attention(q, k, v, lam, subln_g[i], lambda_init)
        y = s5_ssm(u, ssm_a_re[i], ssm_a_im[i], ssm_log_dt[i], ssm_b_re[i], ssm_b_im[i],
                   ssm_c_re[i], ssm_c_im[i], ssm_d[i])
        y = jax.nn.gelu(y)
        g = y @ w_glu[i]
        y = g[..., :D_SSM] * jax.nn.sigmoid(g[..., D_SSM:])
        ssm_out = rms_norm(y, ssm_norm_g[i])
        mix = jnp.concatenate([attn_out, ssm_out], axis=-1) @ w_out[i]
        x = layer_norm(DEEPNORM_ALPHA * x + mix, ln1_g[i], ln1_b[i])
        r = DEEPNORM_ALPHA * x + moe(x, w_router[i], b_router[i], w_gate_up[i], b_gate_up[i],
                                     w_down[i], b_down[i])
        gate = jax.nn.sigmoid(r @ w_ple_gate[i])
        x = layer_norm(r + gate * (p[i] @ w_ple_proj[i]), ln2_g[i], ln2_b[i])
    return x
```

```python
import functools
import math

import jax
import jax.numpy as jnp
from jax import lax
from jax.experimental import pallas as pl
from jax.experimental.pallas import tpu as pltpu

F32 = jnp.float32
BF16 = jnp.bfloat16

N_HEADS = 8
HEAD_DIM = 32
V_DIM = 2 * HEAD_DIM
D_ATTN = N_HEADS * V_DIM
SSM_GROUP = 16
SSM_STATE = 64
N_EXPERTS = 32
TOP_K = 4
TOP_K_SHIFT = 2
SWIGLU_LIMIT = 7.0
SWIGLU_ALPHA = 1.702
LN_EPS = 1e-5
RMS_EPS = 1e-5
DEPTH = 1
DEEPNORM_ALPHA = (2.0 * DEPTH) ** 0.25
LOG2E = math.log2(math.e)

LANES = 128
SUBLANES = 8
VMEM_LIMIT = 56 * 1024 * 1024

PROJ_ROWS = 512
ATTN_Q = 256
SSM_CHUNK = 128
SSM_MM_ROWS = 256
MIX_ROWS = 512
MOE_ROWS = 256
DISPATCH_ROWS = 256
COMBINE_ROWS = 256


def _params(*sem):
    return pltpu.CompilerParams(dimension_semantics=sem, vmem_limit_bytes=VMEM_LIMIT)


def _in_proj_kernel(x_ref, wq_ref, wkt_ref, wv_ref, wu_ref, q_ref, kt_ref, v_ref, u_ref):
    xb = x_ref[0].astype(BF16)
    q = jnp.dot(xb, wq_ref[...], preferred_element_type=F32)
    q_ref[0] = (q * (HEAD_DIM ** -0.5 * LOG2E)).astype(BF16)
    kt = lax.dot_general(wkt_ref[...], xb, (((1,), (1,)), ((), ())),
                         preferred_element_type=F32)
    kt_ref[0, 0] = kt.astype(BF16)
    v_ref[0] = jnp.dot(xb, wv_ref[...], preferred_element_type=F32).astype(BF16)
    u_ref[0] = jnp.dot(xb, wu_ref[...], preferred_element_type=F32)


def _in_proj(x, w_in):
    B, L, D = x.shape
    d_ssm = w_in.shape[1] - 3 * D_ATTN
    tm = PROJ_ROWS
    nt = L // tm
    wb = w_in.astype(BF16)
    wq = wb[:, :D_ATTN]
    wkt = wb[:, D_ATTN:2 * D_ATTN].T
    wv = wb[:, 2 * D_ATTN:3 * D_ATTN]
    wu = wb[:, 3 * D_ATTN:]
    full = lambda shape: pl.BlockSpec(shape, lambda b, i: (0,) * len(shape))
    return pl.pallas_call(
        _in_proj_kernel,
        grid=(B, nt),
        in_specs=[pl.BlockSpec((1, tm, D), lambda b, i: (b, i, 0)),
                  full((D, D_ATTN)), full((D_ATTN, D)), full((D, D_ATTN)), full((D, d_ssm))],
        out_specs=[pl.BlockSpec((1, tm, D_ATTN), lambda b, i: (b, i, 0)),
                   pl.BlockSpec((1, 1, D_ATTN, tm), lambda b, i: (b, i, 0, 0)),
                   pl.BlockSpec((1, tm, D_ATTN), lambda b, i: (b, i, 0)),
                   pl.BlockSpec((1, tm, d_ssm), lambda b, i: (b, i, 0))],
        out_shape=[jax.ShapeDtypeStruct((B, L, D_ATTN), BF16),
                   jax.ShapeDtypeStruct((B, nt, D_ATTN, tm), BF16),
                   jax.ShapeDtypeStruct((B, L, D_ATTN), BF16),
                   jax.ShapeDtypeStruct((B, L, d_ssm), F32)],
        compiler_params=_params("parallel", "parallel"),
        name="in_proj",
    )(x, wq, wkt, wv, wu)


def _attn_kernel(slope_ref, lam_ref, q_ref, kt_ref, v_ref, g_ref, o_ref,
                 qm_ref, m_ref, l_ref, acc_ref, *, tq, tk, lambda_init):
    hp = pl.program_id(1)
    qi = pl.program_id(2)
    q0 = qi * tq

    q = q_ref[0]
    lane_q = lax.broadcasted_iota(jnp.int32, q.shape, 1)
    for c in range(4):
        qm_ref[c] = jnp.where(lane_q // HEAD_DIM == c, q, jnp.zeros_like(q))
        m_ref[c] = jnp.full((tq, 1), -jnp.inf, F32)
        l_ref[c] = jnp.zeros((tq, 1), F32)
        acc_ref[c] = jnp.zeros((tq, LANES), F32)

    col = lax.broadcasted_iota(jnp.int32, (1, tk), 1)
    row = lax.broadcasted_iota(jnp.int32, (tq, 1), 0)

    def tile(t, masked):
        kt = kt_ref[0, t]
        v = v_ref[0, pl.ds(pl.multiple_of(t * tk, tk), tk), :]
        rel = t * tk - q0 + col
        relf = rel.astype(F32)
        for j in range(2):
            brow = (slope_ref[2 * hp + j] * LOG2E) * relf
            for mi in range(2):
                c = 2 * j + mi
                s = jnp.dot(qm_ref[c], kt, preferred_element_type=F32) + brow
                if masked:
                    s = jnp.where(rel <= row, s, -jnp.inf)
                m_old = m_ref[c]
                m_new = jnp.maximum(m_old, jnp.max(s, axis=1, keepdims=True))
                p = jnp.exp2(s - m_new)
                alpha = jnp.exp2(m_old - m_new)
                l_ref[c] = alpha * l_ref[c] + jnp.sum(p, axis=1, keepdims=True)
                acc_ref[c] = alpha * acc_ref[c] + jnp.dot(p.astype(BF16), v,
                                                          preferred_element_type=F32)
                m_ref[c] = m_new

    n_full = q0 // tk

    def body(t, carry):
        tile(t, False)
        return carry

    lax.fori_loop(0, n_full, body, 0)
    tile(n_full, True)

    lam = lam_ref[...]
    lane = lax.broadcasted_iota(jnp.int32, (1, LANES), 1)
    outs = []
    for j in range(2):
        d = acc_ref[2 * j] / l_ref[2 * j] - lam * (acc_ref[2 * j + 1] / l_ref[2 * j + 1])
        in_head = lane // V_DIM == j
        ms = jnp.sum(jnp.where(in_head, d * d, 0.0), axis=1, keepdims=True) * (1.0 / V_DIM)
        outs.append(d * lax.rsqrt(ms + RMS_EPS))
    out = jnp.where(lane < V_DIM, outs[0], outs[1]) * g_ref[...] * (1.0 - lambda_init)
    o_ref[0] = out.astype(o_ref.dtype)


def _attention(q, kt, v, lam_vecs, subln_g, lambda_init):
    B, L, _ = q.shape
    tq, tk = ATTN_Q, PROJ_ROWS
    nk = L // tk
    slopes = jnp.exp2(-(jnp.arange(N_HEADS, dtype=F32) + 1.0) * (8.0 / N_HEADS))
    lam = (jnp.exp(jnp.sum(lam_vecs[0] * lam_vecs[1])) - jnp.exp(jnp.sum(lam_vecs[2] * lam_vecs[3]))
           + lambda_init).reshape(1, 1).astype(F32)
    g2 = jnp.tile(subln_g.astype(F32), 2).reshape(1, LANES)
    kern = functools.partial(_attn_kernel, tq=tq, tk=tk, lambda_init=lambda_init)
    return pl.pallas_call(
        kern,
        grid_spec=pltpu.PrefetchScalarGridSpec(
            num_scalar_prefetch=1,
            grid=(B, N_HEADS // 2, L // tq),
            in_specs=[pl.BlockSpec((1, 1), lambda b, h, i, s: (0, 0)),
                      pl.BlockSpec((1, tq, LANES), lambda b, h, i, s: (b, i, h)),
                      pl.BlockSpec((1, nk, LANES, tk), lambda b, h, i, s: (b, 0, h, 0)),
                      pl.BlockSpec((1, L, LANES), lambda b, h, i, s: (b, 0, h)),
                      pl.BlockSpec((1, LANES), lambda b, h, i, s: (0, 0))],
            out_specs=pl.BlockSpec((1, tq, LANES), lambda b, h, i, s: (b, i, h)),
            scratch_shapes=[pltpu.VMEM((4, tq, LANES), BF16),
                            pltpu.VMEM((4, tq, 1), F32),
                            pltpu.VMEM((4, tq, 1), F32),
                            pltpu.VMEM((4, tq, LANES), F32)]),
        out_shape=jax.ShapeDtypeStruct((B, L, D_ATTN), BF16),
        compiler_params=_params("parallel", "parallel", "arbitrary"),
        name="diff_attn",
    )(slopes, lam, q, kt, v, g2)


def _ssm_fold_params(a_re, a_im, log_dt, b_re, b_im, c_re, c_im, batch):
    G, P = a_re.shape
    C = b_re.shape[-1]
    half = G // 2
    A = lax.complex(a_re.astype(F32), a_im.astype(F32))
    dt = jnp.exp(log_dt.astype(F32))[:, None]
    a_bar = jnp.exp(A * dt)
    b_bar = ((a_bar - 1.0) / A)[..., None] * lax.complex(b_re.astype(F32), b_im.astype(F32))
    sel = jnp.eye(half, dtype=F32)[jnp.arange(G) % half]
    fold_b = lambda m: jnp.einsum('gpc,gk->gckp', m, sel).reshape(G * C, half * P)
    b_fold = jnp.concatenate([fold_b(jnp.real(b_bar)), fold_b(jnp.imag(b_bar))], axis=1)
    fold_c = lambda m: jnp.einsum('gcp,gk->kpgc', m, sel).reshape(half * P, G * C)
    c_fold = jnp.concatenate([fold_c(c_re.astype(F32)), -fold_c(c_im.astype(F32))], axis=0)
    a_rows = jnp.tile(a_bar.reshape(2, half * P), (batch, 1))
    a_tile = jnp.concatenate([jnp.real(a_rows), jnp.imag(a_rows)], axis=1)
    return b_fold.astype(BF16), c_fold.astype(BF16), a_tile.astype(F32)


def _gelu_tanh(x):
    return 0.5 * x * (1.0 + jnp.tanh(math.sqrt(2.0 / math.pi) * (x + 0.044715 * (x * x * x))))


def _ssm_kernel(u_ref, bf_ref, cf_ref, a_ref, d_ref, wglu_ref, gn_ref, o_ref,
                lhs_ref, x_ref, y_ref, s_ref, *, chunk, batch):
    i = pl.program_id(0)
    d_ssm = u_ref.shape[-1]
    n_slab = d_ssm // LANES
    ns = x_ref.shape[-1] // 2
    rows = SUBLANES

    @pl.when(i == 0)
    def _():
        s_ref[...] = jnp.zeros_like(s_ref)

    zero = jnp.zeros((chunk, LANES), F32)
    for b in range(batch):
        ub = u_ref[b]
        for h in range(2):
            for c in range(n_slab):
                src = ub[:, c * LANES:(c + 1) * LANES] if (c * 2) // n_slab == h else zero
                lhs_ref[c, pl.ds(2 * b + h, chunk, stride=rows), :] = src
    n_blk = (rows * chunk) // SSM_MM_ROWS

    def in_mm(rb, carry):
        rs = pl.ds(pl.multiple_of(rb * SSM_MM_ROWS, SSM_MM_ROWS), SSM_MM_ROWS)
        lhs = jnp.concatenate([lhs_ref[c, rs, :] for c in range(n_slab)], axis=1).astype(BF16)
        x_ref[rs, :] = jnp.dot(lhs, bf_ref[...], preferred_element_type=F32)
        return carry

    lax.fori_loop(0, n_blk, in_mm, 0)

    a_re = a_ref[:, :ns]
    a_im = a_ref[:, ns:]

    def step(t, carry):
        s_re, s_im = carry
        r0 = pl.multiple_of(t * rows, rows)
        n_re = a_re * s_re - a_im * s_im + x_ref[pl.ds(r0, rows), :ns]
        n_im = a_re * s_im + a_im * s_re + x_ref[pl.ds(r0, rows), ns:]
        x_ref[pl.ds(r0, rows), :ns] = n_re
        x_ref[pl.ds(r0, rows), ns:] = n_im
        return n_re, n_im

    s_re, s_im = lax.fori_loop(0, chunk, step, (s_ref[:, :ns], s_ref[:, ns:]), unroll=2)
    s_ref[:, :ns] = s_re
    s_ref[:, ns:] = s_im

    def out_mm(rb, carry):
        rs = pl.ds(pl.multiple_of(rb * SSM_MM_ROWS, SSM_MM_ROWS), SSM_MM_ROWS)
        y = jnp.dot(x_ref[rs, :].astype(BF16), cf_ref[...], preferred_element_type=F32)
        for c in range(n_slab):
            y_ref[c, rs, :] = y[:, c * LANES:(c + 1) * LANES]
        return carry

    lax.fori_loop(0, n_blk, out_mm, 0)

    for b in range(batch):
        parts = []
        for c in range(n_slab):
            h = (c * 2) // n_slab
            parts.append(y_ref[c, pl.ds(2 * b + h, chunk, stride=rows), :])
        yb = jnp.concatenate(parts, axis=1) + d_ref[...] * u_ref[b]
        g = jnp.dot(_gelu_tanh(yb).astype(BF16), wglu_ref[...], preferred_element_type=F32)
        z = g[:, :d_ssm] * jax.nn.sigmoid(g[:, d_ssm:])
        ms = jnp.mean(z * z, axis=1, keepdims=True)
        o_ref[b] = (z * lax.rsqrt(ms + RMS_EPS) * gn_ref[...]).astype(o_ref.dtype)


def _ssm(u, b_fold, c_fold, a_tile, d_skip, w_glu, norm_g):
    B, L, d_ssm = u.shape
    assert 2 * B == SUBLANES, "row-stream layout packs batch x 2 group halves on 8 sublanes"
    chunk = SSM_CHUNK
    ns2 = a_tile.shape[1]
    n_slab = d_ssm // LANES
    full = lambda shape: pl.BlockSpec(shape, lambda i: (0,) * len(shape))
    kern = functools.partial(_ssm_kernel, chunk=chunk, batch=B)
    return pl.pallas_call(
        kern,
        grid=(L // chunk,),
        in_specs=[pl.BlockSpec((B, chunk, d_ssm), lambda i: (0, i, 0)),
                  full((d_ssm, ns2)), full((ns2, d_ssm)), full((SUBLANES, ns2)),
                  full((1, d_ssm)), full((d_ssm, 2 * d_ssm)), full((1, d_ssm))],
        out_specs=pl.BlockSpec((B, chunk, d_ssm), lambda i: (0, i, 0)),
        out_shape=jax.ShapeDtypeStruct((B, L, d_ssm), BF16),
        scratch_shapes=[pltpu.VMEM((n_slab, SUBLANES * chunk, LANES), F32),
                        pltpu.VMEM((SUBLANES * chunk, ns2), F32),
                        pltpu.VMEM((n_slab, SUBLANES * chunk, LANES), F32),
                        pltpu.VMEM((SUBLANES, ns2), F32)],
        compiler_params=_params("arbitrary"),
        name="s5_ssm",
    )(u, b_fold, c_fold, a_tile, d_skip.reshape(1, d_ssm).astype(F32), w_glu.astype(BF16),
      norm_g.reshape(1, d_ssm).astype(F32))


def _layer_norm(x, g, b):
    mu = jnp.mean(x, axis=1, keepdims=True)
    xc = x - mu
    var = jnp.mean(xc * xc, axis=1, keepdims=True)
    return xc * lax.rsqrt(var + LN_EPS) * g + b


def _slab_load(ref, n_rows, d, lead=()):
    return jnp.concatenate(
        [ref[lead + (pl.ds(c, n_rows, stride=SUBLANES), slice(None))] for c in range(d // LANES)],
        axis=1)


def _slab_store(ref, val):
    n_rows, d = val.shape
    for c in range(d // LANES):
        ref[pl.ds(c, n_rows, stride=SUBLANES), :] = val[:, c * LANES:(c + 1) * LANES]


def _mix_kernel(x_ref, attn_ref, ssm_ref, woa_ref, wos_ref, g_ref, b_ref, wrt_ref, br_ref,
                x1_ref, idx_ref, gate_ref, rank_ref, cnt_ref, carry_ref):
    i = pl.program_id(0)
    tm = x_ref.shape[0]

    @pl.when(i == 0)
    def _():
        carry_ref[...] = jnp.zeros_like(carry_ref)

    mix = (jnp.dot(attn_ref[...], woa_ref[...], preferred_element_type=F32)
           + jnp.dot(ssm_ref[...], wos_ref[...], preferred_element_type=F32))
    x1 = _layer_norm(DEEPNORM_ALPHA * x_ref[...] + mix, g_ref[...], b_ref[...])
    _slab_store(x1_ref, x1)

    logits = lax.dot_general(wrt_ref[...], x1.astype(BF16), (((1,), (1,)), ((), ())),
                             preferred_element_type=F32) + br_ref[...]
    e_iota = lax.broadcasted_iota(jnp.int32, logits.shape, 0).astype(F32)
    work = logits
    vals, hots = [], []
    for k in range(TOP_K):
        mx = jnp.max(work, axis=0, keepdims=True)
        idx = jnp.min(jnp.where(work == mx, e_iota, float(N_EXPERTS)), axis=0, keepdims=True)
        hot = e_iota == idx
        work = jnp.where(hot, -jnp.inf, work)
        vals.append(mx)
        hots.append(hot)
        idx_ref[k:k + 1, :] = idx.astype(jnp.int32)
    ex = [jnp.exp(v - vals[0]) for v in vals]
    den = ex[0] + ex[1] + ex[2] + ex[3]
    for k in range(TOP_K):
        gate_ref[k:k + 1, :] = ex[k] / den

    sel = (hots[0] | hots[1] | hots[2] | hots[3]).astype(F32)
    r_i = lax.broadcasted_iota(jnp.int32, (tm, tm), 0)
    c_i = lax.broadcasted_iota(jnp.int32, (tm, tm), 1)
    before = (r_i < c_i).astype(BF16)
    prior = jnp.dot(sel.astype(BF16), before, preferred_element_type=F32) + carry_ref[:, 0:1]
    for k in range(TOP_K):
        rank_ref[k:k + 1, :] = jnp.sum(jnp.where(hots[k], prior, 0.0), axis=0,
                                       keepdims=True).astype(jnp.int32)
    carry_ref[...] = carry_ref[...] + jnp.sum(sel, axis=1, keepdims=True)
    cnt_ref[...] = carry_ref[...]


def _mix_route(x2, attn2, ssm2, w_out, ln_g, ln_b, w_router, b_router):
    T, D = x2.shape
    tm = MIX_ROWS
    d_attn = attn2.shape[1]
    wo = w_out.astype(BF16)
    full = lambda shape: pl.BlockSpec(shape, lambda i: (0,) * len(shape))
    rowblk = lambda w: pl.BlockSpec((tm, w), lambda i: (i, 0))
    colblk = pl.BlockSpec((TOP_K, tm), lambda i: (0, i))
    return pl.pallas_call(
        _mix_kernel,
        grid=(T // tm,),
        in_specs=[rowblk(D), rowblk(d_attn), rowblk(ssm2.shape[1]),
                  full((d_attn, D)), full((D - d_attn, D)), full((1, D)), full((1, D)),
                  full((N_EXPERTS, D)), full((N_EXPERTS, 1))],
        out_specs=[pl.BlockSpec((tm * SUBLANES, LANES), lambda i: (i, 0)),
                   colblk, colblk, colblk, full((N_EXPERTS, LANES))],
        out_shape=[jax.ShapeDtypeStruct((T * SUBLANES, LANES), F32),
                   jax.ShapeDtypeStruct((TOP_K, T), jnp.int32),
                   jax.ShapeDtypeStruct((TOP_K, T), F32),
                   jax.ShapeDtypeStruct((TOP_K, T), jnp.int32),
                   jax.ShapeDtypeStruct((N_EXPERTS, LANES), F32)],
        scratch_shapes=[pltpu.VMEM((N_EXPERTS, LANES), F32)],
        compiler_params=_params("arbitrary"),
        name="mix_route",
    )(x2, attn2, ssm2, wo[:d_attn], wo[d_attn:], ln_g.reshape(1, D).astype(F32),
      ln_b.reshape(1, D).astype(F32), w_router.T.astype(BF16),
      b_router.reshape(N_EXPERTS, 1).astype(F32))


def _tok_rows(r):
    return pl.ds(pl.multiple_of(lax.shift_right_logical(r, TOP_K_SHIFT) * SUBLANES, SUBLANES),
                 SUBLANES)


def _slot_rows(dest_ref, r):
    return pl.ds(pl.multiple_of(dest_ref[r], SUBLANES), SUBLANES)


def _dispatch_kernel(zs_ref, nu_ref, dest_ref, x_ref, xs_ref, zbuf_ref, sem, zsem):
    i = pl.program_id(0)
    n = dest_ref.shape[0]
    blk_rows = MOE_ROWS * SUBLANES
    n_blocks = xs_ref.shape[0] // blk_rows

    def zero_copy(start):
        dst = xs_ref.at[pl.ds(pl.multiple_of(start, blk_rows), blk_rows)]
        return pltpu.make_async_copy(zbuf_ref, dst, zsem)

    @pl.when(i == 0)
    def _():
        zbuf_ref[...] = jnp.zeros_like(zbuf_ref)
        for e in range(N_EXPERTS):
            @pl.when(zs_ref[e] >= 0)
            def _():
                zero_copy(zs_ref[e]).start()

        def tail_start(b, c):
            zero_copy(b * blk_rows).start()
            return c

        def tail_wait(b, c):
            zero_copy(b * blk_rows).wait()
            return c

        lax.fori_loop(nu_ref[0], n_blocks, tail_start, 0)
        for e in range(N_EXPERTS):
            @pl.when(zs_ref[e] >= 0)
            def _():
                zero_copy(zs_ref[e]).wait()
        lax.fori_loop(nu_ref[0], n_blocks, tail_wait, 0)

    def row_copy(r):
        return pltpu.make_async_copy(x_ref.at[_tok_rows(r)], xs_ref.at[_slot_rows(dest_ref, r)],
                                     sem)

    def issue(r, c):
        row_copy(r).start()
        return c

    def drain(r, c):
        row_copy(r).wait()
        return c

    lax.fori_loop(0, n, issue, 0, unroll=8)
    lax.fori_loop(0, n, drain, 0, unroll=8)


def _dispatch(x1s, dest8, zero_start8, n_used, n_pad):
    tm = DISPATCH_ROWS
    n_tok = x1s.shape[0] // SUBLANES
    return pl.pallas_call(
        _dispatch_kernel,
        grid_spec=pltpu.PrefetchScalarGridSpec(
            num_scalar_prefetch=2,
            grid=(n_tok // tm,),
            in_specs=[pl.BlockSpec((tm * TOP_K,), lambda i, z, u: (i,), memory_space=pltpu.SMEM),
                      pl.BlockSpec((tm * SUBLANES, LANES), lambda i, z, u: (i, 0))],
            out_specs=pl.BlockSpec(memory_space=pl.ANY),
            scratch_shapes=[pltpu.VMEM((MOE_ROWS * SUBLANES, LANES), F32),
                            pltpu.SemaphoreType.DMA, pltpu.SemaphoreType.DMA]),
        out_shape=jax.ShapeDtypeStruct((n_pad * SUBLANES, LANES), F32),
        compiler_params=_params("arbitrary"),
        name="moe_dispatch",
    )(zero_start8, n_used, dest8, x1s)


def _expert_kernel(be_ref, nu_ref, xs_ref, wg_ref, wl_ref, bg_ref, bl_ref, wd_ref, bd_ref, o_ref):
    i = pl.program_id(0)
    d = wg_ref.shape[1]

    @pl.when(i < nu_ref[0])
    def _():
        xb = _slab_load(xs_ref, MOE_ROWS, d).astype(BF16)
        hg = jnp.dot(xb, wg_ref[0], preferred_element_type=F32) + bg_ref[0]
        hl = jnp.dot(xb, wl_ref[0], preferred_element_type=F32) + bl_ref[0]
        xg = jnp.minimum(hg, SWIGLU_LIMIT)
        xl = jnp.clip(hl, -SWIGLU_LIMIT, SWIGLU_LIMIT)
        act = xg * jax.nn.sigmoid(SWIGLU_ALPHA * xg) * (xl + 1.0)
        out = jnp.dot(act.astype(BF16), wd_ref[0], preferred_element_type=F32) + bd_ref[0]
        _slab_store(o_ref, out)

    @pl.when(i >= nu_ref[0])
    def _():
        o_ref[...] = jnp.zeros_like(o_ref)


def _experts(xs, block_e, n_used, w_gate_up, b_gate_up, w_down, b_down):
    E, D, F2 = w_gate_up.shape
    F = F2 // 2
    tm = MOE_ROWS
    n_blocks = xs.shape[0] // (tm * SUBLANES)
    wg = w_gate_up[:, :, 0::2].astype(BF16)
    wl = w_gate_up[:, :, 1::2].astype(BF16)
    bg = b_gate_up[:, 0::2].reshape(E, 1, F).astype(F32)
    bl = b_gate_up[:, 1::2].reshape(E, 1, F).astype(F32)
    wd = w_down.astype(BF16)
    bd = b_down.reshape(E, 1, D).astype(F32)
    blk = lambda i, be, nu: (jnp.minimum(i, nu[0] - 1), 0)
    wmap = lambda i, be, nu: (be[i], 0, 0)
    return pl.pallas_call(
        _expert_kernel,
        grid_spec=pltpu.PrefetchScalarGridSpec(
            num_scalar_prefetch=2,
            grid=(n_blocks,),
            in_specs=[pl.BlockSpec((tm * SUBLANES, LANES), blk),
                      pl.BlockSpec((1, D, F), wmap), pl.BlockSpec((1, D, F), wmap),
                      pl.BlockSpec((1, 1, F), wmap), pl.BlockSpec((1, 1, F), wmap),
                      pl.BlockSpec((1, F, D), wmap), pl.BlockSpec((1, 1, D), wmap)],
            out_specs=pl.BlockSpec((tm * SUBLANES, LANES), lambda i, be, nu: (i, 0))),
        out_shape=jax.ShapeDtypeStruct(xs.shape, F32),
        compiler_params=_params("arbitrary"),
        name="moe_experts",
    )(block_e, n_used, xs, wg, wl, bg, bl, wd, bd)


def _combine_kernel(dest_ref, gate_ref, x1_ref, p_ref, wpg_ref, wpp_ref, g_ref, b_ref, os_ref,
                    o_ref, rows_ref, sem):
    n = dest_ref.shape[0]
    tm, d = o_ref.shape

    def row_copy(r):
        return pltpu.make_async_copy(os_ref.at[_slot_rows(dest_ref, r)],
                                     rows_ref.at[r & (TOP_K - 1), _tok_rows(r)], sem)

    def issue(r, c):
        row_copy(r).start()
        return c

    def drain(r, c):
        row_copy(r).wait()
        return c

    lax.fori_loop(0, n, issue, 0, unroll=8)
    pp = jnp.dot(p_ref[...].astype(BF16), wpp_ref[...], preferred_element_type=F32)
    lax.fori_loop(0, n, drain, 0, unroll=8)

    gates = gate_ref[...]
    moe = gates[:, 0:1] * _slab_load(rows_ref, tm, d, lead=(0,))
    for k in range(1, TOP_K):
        moe = moe + gates[:, k:k + 1] * _slab_load(rows_ref, tm, d, lead=(k,))
    r = DEEPNORM_ALPHA * _slab_load(x1_ref, tm, d) + moe
    gate = jax.nn.sigmoid(jnp.dot(r.astype(BF16), wpg_ref[...], preferred_element_type=F32))
    o_ref[...] = _layer_norm(r + gate * pp, g_ref[...], b_ref[...])


def _combine(dest8, gates_t, x1s, p2, w_ple_gate, w_ple_proj, ln_g, ln_b, out_sorted):
    T, pd = p2.shape
    D = w_ple_gate.shape[0]
    tm = COMBINE_ROWS
    full = lambda shape: pl.BlockSpec(shape, lambda i: (0,) * len(shape))
    return pl.pallas_call(
        _combine_kernel,
        grid=(T // tm,),
        in_specs=[pl.BlockSpec((tm * TOP_K,), lambda i: (i,), memory_space=pltpu.SMEM),
                  pl.BlockSpec((tm, TOP_K), lambda i: (i, 0)),
                  pl.BlockSpec((tm * SUBLANES, LANES), lambda i: (i, 0)),
                  pl.BlockSpec((tm, pd), lambda i: (i, 0)),
                  full((D, D)), full((pd, D)), full((1, D)), full((1, D)),
                  pl.BlockSpec(memory_space=pl.ANY)],
        out_specs=pl.BlockSpec((tm, D), lambda i: (i, 0)),
        out_shape=jax.ShapeDtypeStruct((T, D), F32),
        scratch_shapes=[pltpu.VMEM((TOP_K, tm * SUBLANES, LANES), F32), pltpu.SemaphoreType.DMA],
        compiler_params=_params("arbitrary"),
        name="moe_combine",
    )(dest8, gates_t, x1s, p2, w_ple_gate.astype(BF16), w_ple_proj.astype(BF16),
      ln_g.reshape(1, D).astype(F32), ln_b.reshape(1, D).astype(F32), out_sorted)


def kernel(x, p, w_in, lambda_q1, lambda_k1, lambda_q2, lambda_k2, subln_g, ssm_a_re, ssm_a_im,
           ssm_log_dt, ssm_b_re, ssm_b_im, ssm_c_re, ssm_c_im, ssm_d, w_glu, ssm_norm_g, w_out,
           ln1_g, ln1_b, w_router, b_router, w_gate_up, b_gate_up, w_down, b_down, w_ple_gate,
           w_ple_proj, ln2_g, ln2_b):
    B, L, D = x.shape
    T = B * L
    assert D == SUBLANES * LANES, "token-slab layout holds one token per (8, 128) tile"
    for i in range(DEPTH):
        lambda_init = 0.8 - 0.6 * math.exp(-0.3 * i)
        q, kt, v, u = _in_proj(x, w_in[i])
        lam_vecs = jnp.stack([lambda_q1[i], lambda_k1[i], lambda_q2[i], lambda_k2[i]]).astype(F32)
        attn = _attention(q, kt, v, lam_vecs, subln_g[i], lambda_init)
        b_fold, c_fold, a_tile = _ssm_fold_params(ssm_a_re[i], ssm_a_im[i], ssm_log_dt[i],
                                                  ssm_b_re[i], ssm_b_im[i], ssm_c_re[i],
                                                  ssm_c_im[i], B)
        ssm = _ssm(u, b_fold, c_fold, a_tile, ssm_d[i], w_glu[i], ssm_norm_g[i])

        x1, idx, gates, rank, counts = _mix_route(
            x.reshape(T, D), attn.reshape(T, -1), ssm.reshape(T, -1), w_out[i], ln1_g[i],
            ln1_b[i], w_router[i], b_router[i])

        cnt = counts[:, 0].astype(jnp.int32)
        padded = ((cnt + MOE_ROWS - 1) // MOE_ROWS) * MOE_ROWS
        pad_end = jnp.cumsum(padded)
        pad_start = pad_end - padded
        n_blocks = (T * TOP_K) // MOE_ROWS + N_EXPERTS
        n_pad = n_blocks * MOE_ROWS
        dest8 = ((pad_start[idx] + rank) * SUBLANES).T.reshape(-1)
        block_e = jnp.minimum(
            jnp.searchsorted(pad_end, jnp.arange(n_blocks, dtype=jnp.int32) * MOE_ROWS,
                             side='right'), N_EXPERTS - 1).astype(jnp.int32)
        n_used = (pad_end[-1:] // MOE_ROWS).astype(jnp.int32)
        zero_start8 = jnp.where(padded > 0, (pad_end - MOE_ROWS) * SUBLANES, -1).astype(jnp.int32)

        xs = _dispatch(x1, dest8, zero_start8, n_used, n_pad)
        out_sorted = _experts(xs, block_e, n_used, w_gate_up[i], b_gate_up[i], w_down[i],
                              b_down[i])
        x = _combine(dest8, gates.T, x1, p[i].reshape(T, -1), w_ple_gate[i], w_ple_proj[i],
                     ln2_g[i], ln2_b[i], out_sorted).reshape(B, L, D)
    return x
```

```python
import functools
import math

import jax
import jax.numpy as jnp
from jax import lax
from jax.experimental import pallas as pl
from jax.experimental.pallas import tpu as pltpu

F32 = jnp.float32
BF16 = jnp.bfloat16

N_HEADS = 8
HEAD_DIM = 32
V_DIM = 2 * HEAD_DIM
D_ATTN = N_HEADS * V_DIM
SSM_GROUP = 16
SSM_STATE = 64
N_EXPERTS = 32
TOP_K = 4
TOP_K_SHIFT = 2
SWIGLU_LIMIT = 7.0
SWIGLU_ALPHA = 1.702
LN_EPS = 1e-5
RMS_EPS = 1e-5
DEPTH = 1
DEEPNORM_ALPHA = (2.0 * DEPTH) ** 0.25
LOG2E = math.log2(math.e)

LANES = 128
SUBLANES = 8
VMEM_LIMIT = 56 * 1024 * 1024

PROJ_ROWS = 512
ATTN_Q = 512
SSM_CHUNK = 128
SSM_MM_ROWS = 256
MIX_ROWS = 512
MOE_ROWS = 256
SPLIT_COLS = 512
WPREP_ROWS = 256
DISPATCH_ROWS = 256
COMBINE_ROWS = 256


def _params(*sem):
    return pltpu.CompilerParams(dimension_semantics=sem, vmem_limit_bytes=VMEM_LIMIT)


def _in_proj_kernel(x_ref, wq_ref, wkt_ref, wv_ref, wu_ref, q_ref, kt_ref, v_ref, u_ref):
    xb = x_ref[0].astype(BF16)
    q = jnp.dot(xb, wq_ref[...], preferred_element_type=F32)
    q_ref[0] = (q * (HEAD_DIM ** -0.5 * LOG2E)).astype(BF16)
    kt = lax.dot_general(wkt_ref[...], xb, (((1,), (1,)), ((), ())),
                         preferred_element_type=F32)
    kt_ref[0, 0] = kt.astype(BF16)
    v_ref[0] = jnp.dot(xb, wv_ref[...], preferred_element_type=F32).astype(BF16)
    u_ref[0] = jnp.dot(xb, wu_ref[...], preferred_element_type=F32)


def _in_proj(x, w_in):
    B, L, D = x.shape
    d_ssm = w_in.shape[1] - 3 * D_ATTN
    tm = PROJ_ROWS
    nt = L // tm
    wb = w_in.astype(BF16)
    wq = wb[:, :D_ATTN]
    wkt = wb[:, D_ATTN:2 * D_ATTN].T
    wv = wb[:, 2 * D_ATTN:3 * D_ATTN]
    wu = wb[:, 3 * D_ATTN:]
    full = lambda shape: pl.BlockSpec(shape, lambda b, i: (0,) * len(shape))
    return pl.pallas_call(
        _in_proj_kernel,
        grid=(B, nt),
        in_specs=[pl.BlockSpec((1, tm, D), lambda b, i: (b, i, 0)),
                  full((D, D_ATTN)), full((D_ATTN, D)), full((D, D_ATTN)), full((D, d_ssm))],
        out_specs=[pl.BlockSpec((1, tm, D_ATTN), lambda b, i: (b, i, 0)),
                   pl.BlockSpec((1, 1, D_ATTN, tm), lambda b, i: (b, i, 0, 0)),
                   pl.BlockSpec((1, tm, D_ATTN), lambda b, i: (b, i, 0)),
                   pl.BlockSpec((1, tm, d_ssm), lambda b, i: (b, i, 0))],
        out_shape=[jax.ShapeDtypeStruct((B, L, D_ATTN), BF16),
                   jax.ShapeDtypeStruct((B, nt, D_ATTN, tm), BF16),
                   jax.ShapeDtypeStruct((B, L, D_ATTN), BF16),
                   jax.ShapeDtypeStruct((B, L, d_ssm), F32)],
        compiler_params=_params("parallel", "parallel"),
        name="in_proj",
    )(x, wq, wkt, wv, wu)


def _attn_kernel(slope_ref, lam_ref, q_ref, kt_ref, v_ref, g_ref, o_ref,
                 q4_ref, p_ref, m_ref, l_ref, acc_ref, *, tq, tk, lambda_init):
    hp = pl.program_id(1)
    qi = pl.program_id(2)
    q0 = qi * tq
    n_cb = tk // LANES

    q = q_ref[0]
    lane_q = lax.broadcasted_iota(jnp.int32, q.shape, 1)
    for c in range(4):
        q4_ref[c * tq:(c + 1) * tq, :] = jnp.where(lane_q // HEAD_DIM == c, q, jnp.zeros_like(q))
    m_ref[...] = jnp.full(m_ref.shape, -jnp.inf, F32)
    l_ref[...] = jnp.zeros(l_ref.shape, F32)
    acc_ref[...] = jnp.zeros(acc_ref.shape, F32)

    col = lax.broadcasted_iota(jnp.int32, (1, tk), 1)
    row = lax.broadcasted_iota(jnp.int32, (tq, 1), 0)

    def tile(t, masked):
        kt = kt_ref[0, t]
        v = v_ref[0, pl.ds(pl.multiple_of(t * tk, tk), tk), :]
        rel = t * tk - q0 + col
        relf = rel.astype(F32)
        s_all = jnp.dot(q4_ref[...], kt, preferred_element_type=F32)
        for c in range(4):
            rs = slice(c * tq, (c + 1) * tq)
            s = s_all[rs] + (slope_ref[2 * hp + c // 2] * LOG2E) * relf
            if masked:
                s = jnp.where(rel <= row, s, -jnp.inf)
            blocks = [s[:, cb * LANES:(cb + 1) * LANES] for cb in range(n_cb)]
            part = blocks[0]
            for blk in blocks[1:]:
                part = jnp.maximum(part, blk)
            m_old = m_ref[rs]
            m_new = jnp.maximum(m_old, jnp.max(part, axis=1, keepdims=True))
            m_ref[rs] = m_new
            alpha = jnp.exp2(m_old - m_new)
            ps = [jnp.exp2(blk - m_new) for blk in blocks]
            psum = ps[0]
            for pb in ps[1:]:
                psum = psum + pb
            l_ref[rs] = alpha * l_ref[rs] + psum
            acc_ref[rs] = alpha * acc_ref[rs]
            p_ref[rs] = jnp.concatenate(ps, axis=1).astype(BF16)
        acc_ref[...] += jnp.dot(p_ref[...], v, preferred_element_type=F32)

    n_full = q0 // tk

    def body(t, carry):
        tile(t, False)
        return carry

    lax.fori_loop(0, n_full, body, 0)
    tile(n_full, True)

    lam = lam_ref[...]
    lane = lax.broadcasted_iota(jnp.int32, (1, LANES), 1)
    outs = []
    for j in range(2):
        o = []
        for c in (2 * j, 2 * j + 1):
            rs = slice(c * tq, (c + 1) * tq)
            o.append(acc_ref[rs] / jnp.sum(l_ref[rs], axis=1, keepdims=True))
        d = o[0] - lam * o[1]
        in_head = lane // V_DIM == j
        ms = jnp.sum(jnp.where(in_head, d * d, 0.0), axis=1, keepdims=True) * (1.0 / V_DIM)
        outs.append(d * lax.rsqrt(ms + RMS_EPS))
    out = jnp.where(lane < V_DIM, outs[0], outs[1]) * g_ref[...] * (1.0 - lambda_init)
    o_ref[0] = out.astype(o_ref.dtype)


def _attention(q, kt, v, lam_vecs, subln_g, lambda_init):
    B, L, _ = q.shape
    tq, tk = ATTN_Q, PROJ_ROWS
    nk = L // tk
    slopes = jnp.exp2(-(jnp.arange(N_HEADS, dtype=F32) + 1.0) * (8.0 / N_HEADS))
    lam = (jnp.exp(jnp.sum(lam_vecs[0] * lam_vecs[1])) - jnp.exp(jnp.sum(lam_vecs[2] * lam_vecs[3]))
           + lambda_init).reshape(1, 1).astype(F32)
    g2 = jnp.tile(subln_g.astype(F32), 2).reshape(1, LANES)
    kern = functools.partial(_attn_kernel, tq=tq, tk=tk, lambda_init=lambda_init)
    return pl.pallas_call(
        kern,
        grid_spec=pltpu.PrefetchScalarGridSpec(
            num_scalar_prefetch=1,
            grid=(B, N_HEADS // 2, L // tq),
            in_specs=[pl.BlockSpec((1, 1), lambda b, h, i, s: (0, 0)),
                      pl.BlockSpec((1, tq, LANES), lambda b, h, i, s: (b, i, h)),
                      pl.BlockSpec((1, nk, LANES, tk), lambda b, h, i, s: (b, 0, h, 0)),
                      pl.BlockSpec((1, L, LANES), lambda b, h, i, s: (b, 0, h)),
                      pl.BlockSpec((1, LANES), lambda b, h, i, s: (0, 0))],
            out_specs=pl.BlockSpec((1, tq, LANES), lambda b, h, i, s: (b, i, h)),
            scratch_shapes=[pltpu.VMEM((4 * tq, LANES), BF16),
                            pltpu.VMEM((4 * tq, tk), BF16),
                            pltpu.VMEM((4 * tq, LANES), F32),
                            pltpu.VMEM((4 * tq, LANES), F32),
                            pltpu.VMEM((4 * tq, LANES), F32)]),
        out_shape=jax.ShapeDtypeStruct((B, L, D_ATTN), BF16),
        compiler_params=_params("parallel", "parallel", "arbitrary"),
        name="diff_attn",
    )(slopes, lam, q, kt, v, g2)


def _ssm_fold_params(a_re, a_im, log_dt, b_re, b_im, c_re, c_im, batch):
    G, P = a_re.shape
    C = b_re.shape[-1]
    half = G // 2
    A = lax.complex(a_re.astype(F32), a_im.astype(F32))
    dt = jnp.exp(log_dt.astype(F32))[:, None]
    a_bar = jnp.exp(A * dt)
    b_bar = ((a_bar - 1.0) / A)[..., None] * lax.complex(b_re.astype(F32), b_im.astype(F32))
    sel = jnp.eye(half, dtype=F32)[jnp.arange(G) % half]
    fold_b = lambda m: jnp.einsum('gpc,gk->gckp', m, sel).reshape(G * C, half * P)
    b_fold = jnp.concatenate([fold_b(jnp.real(b_bar)), fold_b(jnp.imag(b_bar))], axis=1)
    fold_c = lambda m: jnp.einsum('gcp,gk->kpgc', m, sel).reshape(half * P, G * C)
    c_fold = jnp.concatenate([fold_c(c_re.astype(F32)), -fold_c(c_im.astype(F32))], axis=0)
    a_rows = jnp.tile(a_bar.reshape(2, half * P), (batch, 1))
    a_tile = jnp.concatenate([jnp.real(a_rows), jnp.imag(a_rows)], axis=1)
    return b_fold.astype(BF16), c_fold.astype(BF16), a_tile.astype(F32)


def _gelu_tanh(x):
    return 0.5 * x * (1.0 + jnp.tanh(math.sqrt(2.0 / math.pi) * (x + 0.044715 * (x * x * x))))


def _ssm_kernel(u_ref, bf_ref, cf_ref, a_ref, d_ref, wglu_ref, gn_ref, o_ref,
                lhs_ref, x_ref, y_ref, s_ref, *, chunk, batch):
    i = pl.program_id(0)
    d_ssm = u_ref.shape[-1]
    n_slab = d_ssm // LANES
    ns = x_ref.shape[-1] // 2
    rows = SUBLANES

    @pl.when(i == 0)
    def _():
        s_ref[...] = jnp.zeros_like(s_ref)

    zero = jnp.zeros((chunk, LANES), F32)
    for b in range(batch):
        ub = u_ref[b]
        for h in range(2):
            for c in range(n_slab):
                src = ub[:, c * LANES:(c + 1) * LANES] if (c * 2) // n_slab == h else zero
                lhs_ref[c, pl.ds(2 * b + h, chunk, stride=rows), :] = src
    n_blk = (rows * chunk) // SSM_MM_ROWS

    def in_mm(rb, carry):
        rs = pl.ds(pl.multiple_of(rb * SSM_MM_ROWS, SSM_MM_ROWS), SSM_MM_ROWS)
        lhs = jnp.concatenate([lhs_ref[c, rs, :] for c in range(n_slab)], axis=1).astype(BF16)
        x_ref[rs, :] = jnp.dot(lhs, bf_ref[...], preferred_element_type=F32)
        return carry

    lax.fori_loop(0, n_blk, in_mm, 0)

    a_re = a_ref[:, :ns]
    a_im = a_ref[:, ns:]

    def step(t, carry):
        s_re, s_im = carry
        r0 = pl.multiple_of(t * rows, rows)
        n_re = a_re * s_re - a_im * s_im + x_ref[pl.ds(r0, rows), :ns]
        n_im = a_re * s_im + a_im * s_re + x_ref[pl.ds(r0, rows), ns:]
        x_ref[pl.ds(r0, rows), :ns] = n_re
        x_ref[pl.ds(r0, rows), ns:] = n_im
        return n_re, n_im

    s_re, s_im = lax.fori_loop(0, chunk, step, (s_ref[:, :ns], s_ref[:, ns:]), unroll=2)
    s_ref[:, :ns] = s_re
    s_ref[:, ns:] = s_im

    def out_mm(rb, carry):
        rs = pl.ds(pl.multiple_of(rb * SSM_MM_ROWS, SSM_MM_ROWS), SSM_MM_ROWS)
        y = jnp.dot(x_ref[rs, :].astype(BF16), cf_ref[...], preferred_element_type=F32)
        for c in range(n_slab):
            y_ref[c, rs, :] = y[:, c * LANES:(c + 1) * LANES]
        return carry

    lax.fori_loop(0, n_blk, out_mm, 0)

    for b in range(batch):
        parts = []
        for c in range(n_slab):
            h = (c * 2) // n_slab
            parts.append(y_ref[c, pl.ds(2 * b + h, chunk, stride=rows), :])
        yb = jnp.concatenate(parts, axis=1) + d_ref[...] * u_ref[b]
        g = jnp.dot(_gelu_tanh(yb).astype(BF16), wglu_ref[...], preferred_element_type=F32)
        z = g[:, :d_ssm] * jax.nn.sigmoid(g[:, d_ssm:])
        ms = jnp.mean(z * z, axis=1, keepdims=True)
        o_ref[b] = (z * lax.rsqrt(ms + RMS_EPS) * gn_ref[...]).astype(o_ref.dtype)


def _ssm(u, b_fold, c_fold, a_tile, d_skip, w_glu, norm_g):
    B, L, d_ssm = u.shape
    assert 2 * B == SUBLANES, "row-stream layout packs batch x 2 group halves on 8 sublanes"
    chunk = SSM_CHUNK
    ns2 = a_tile.shape[1]
    n_slab = d_ssm // LANES
    full = lambda shape: pl.BlockSpec(shape, lambda i: (0,) * len(shape))
    kern = functools.partial(_ssm_kernel, chunk=chunk, batch=B)
    return pl.pallas_call(
        kern,
        grid=(L // chunk,),
        in_specs=[pl.BlockSpec((B, chunk, d_ssm), lambda i: (0, i, 0)),
                  full((d_ssm, ns2)), full((ns2, d_ssm)), full((SUBLANES, ns2)),
                  full((1, d_ssm)), full((d_ssm, 2 * d_ssm)), full((1, d_ssm))],
        out_specs=pl.BlockSpec((B, chunk, d_ssm), lambda i: (0, i, 0)),
        out_shape=jax.ShapeDtypeStruct((B, L, d_ssm), BF16),
        scratch_shapes=[pltpu.VMEM((n_slab, SUBLANES * chunk, LANES), F32),
                        pltpu.VMEM((SUBLANES * chunk, ns2), F32),
                        pltpu.VMEM((n_slab, SUBLANES * chunk, LANES), F32),
                        pltpu.VMEM((SUBLANES, ns2), F32)],
        compiler_params=_params("arbitrary"),
        name="s5_ssm",
    )(u, b_fold, c_fold, a_tile, d_skip.reshape(1, d_ssm).astype(F32), w_glu.astype(BF16),
      norm_g.reshape(1, d_ssm).astype(F32))


def _layer_norm(x, g, b):
    mu = jnp.mean(x, axis=1, keepdims=True)
    xc = x - mu
    var = jnp.mean(xc * xc, axis=1, keepdims=True)
    return xc * lax.rsqrt(var + LN_EPS) * g + b


def _slab_load(ref, n_rows, d, lead=()):
    return jnp.concatenate(
        [ref[lead + (pl.ds(c, n_rows, stride=SUBLANES), slice(None))] for c in range(d // LANES)],
        axis=1)


def _slab_store(ref, val):
    n_rows, d = val.shape
    for c in range(d // LANES):
        ref[pl.ds(c, n_rows, stride=SUBLANES), :] = val[:, c * LANES:(c + 1) * LANES]


def _mix_kernel(x_ref, attn_ref, ssm_ref, woa_ref, wos_ref, g_ref, b_ref, wrt_ref, br_ref,
                x1_ref, idx_ref, gate_ref, rank_ref, cnt_ref, carry_ref):
    i = pl.program_id(0)
    tm = x_ref.shape[0]

    @pl.when(i == 0)
    def _():
        carry_ref[...] = jnp.zeros_like(carry_ref)

    mix = (jnp.dot(attn_ref[...], woa_ref[...], preferred_element_type=F32)
           + jnp.dot(ssm_ref[...], wos_ref[...], preferred_element_type=F32))
    x1 = _layer_norm(DEEPNORM_ALPHA * x_ref[...] + mix, g_ref[...], b_ref[...])
    _slab_store(x1_ref, x1)

    logits = lax.dot_general(wrt_ref[...], x1.astype(BF16), (((1,), (1,)), ((), ())),
                             preferred_element_type=F32) + br_ref[...]
    e_iota = lax.broadcasted_iota(jnp.int32, logits.shape, 0).astype(F32)
    work = logits
    vals, hots = [], []
    for k in range(TOP_K):
        mx = jnp.max(work, axis=0, keepdims=True)
        idx = jnp.min(jnp.where(work == mx, e_iota, float(N_EXPERTS)), axis=0, keepdims=True)
        hot = e_iota == idx
        work = jnp.where(hot, -jnp.inf, work)
        vals.append(mx)
        hots.append(hot)
        idx_ref[k:k + 1, :] = idx.astype(jnp.int32)
    ex = [jnp.exp(v - vals[0]) for v in vals]
    den = ex[0] + ex[1] + ex[2] + ex[3]
    for k in range(TOP_K):
        gate_ref[k:k + 1, :] = ex[k] / den

    sel = (hots[0] | hots[1] | hots[2] | hots[3]).astype(F32)
    r_i = lax.broadcasted_iota(jnp.int32, (tm, tm), 0)
    c_i = lax.broadcasted_iota(jnp.int32, (tm, tm), 1)
    before = (r_i < c_i).astype(BF16)
    prior = jnp.dot(sel.astype(BF16), before, preferred_element_type=F32) + carry_ref[:, 0:1]
    for k in range(TOP_K):
        rank_ref[k:k + 1, :] = jnp.sum(jnp.where(hots[k], prior, 0.0), axis=0,
                                       keepdims=True).astype(jnp.int32)
    carry_ref[...] = carry_ref[...] + jnp.sum(sel, axis=1, keepdims=True)
    cnt_ref[...] = carry_ref[...]


def _mix_route(x2, attn2, ssm2, w_out, ln_g, ln_b, w_router, b_router):
    T, D = x2.shape
    tm = MIX_ROWS
    d_attn = attn2.shape[1]
    wo = w_out.astype(BF16)
    full = lambda shape: pl.BlockSpec(shape, lambda i: (0,) * len(shape))
    rowblk = lambda w: pl.BlockSpec((tm, w), lambda i: (i, 0))
    colblk = pl.BlockSpec((TOP_K, tm), lambda i: (0, i))
    return pl.pallas_call(
        _mix_kernel,
        grid=(T // tm,),
        in_specs=[rowblk(D), rowblk(d_attn), rowblk(ssm2.shape[1]),
                  full((d_attn, D)), full((D - d_attn, D)), full((1, D)), full((1, D)),
                  full((N_EXPERTS, D)), full((N_EXPERTS, 1))],
        out_specs=[pl.BlockSpec((tm * SUBLANES, LANES), lambda i: (i, 0)),
                   colblk, colblk, colblk, full((N_EXPERTS, LANES))],
        out_shape=[jax.ShapeDtypeStruct((T * SUBLANES, LANES), F32),
                   jax.ShapeDtypeStruct((TOP_K, T), jnp.int32),
                   jax.ShapeDtypeStruct((TOP_K, T), F32),
                   jax.ShapeDtypeStruct((TOP_K, T), jnp.int32),
                   jax.ShapeDtypeStruct((N_EXPERTS, LANES), F32)],
        scratch_shapes=[pltpu.VMEM((N_EXPERTS, LANES), F32)],
        compiler_params=_params("arbitrary"),
        name="mix_route",
    )(x2, attn2, ssm2, wo[:d_attn], wo[d_attn:], ln_g.reshape(1, D).astype(F32),
      ln_b.reshape(1, D).astype(F32), w_router.T.astype(BF16),
      b_router.reshape(N_EXPERTS, 1).astype(F32))


def _tok_rows(r):
    return pl.ds(pl.multiple_of(lax.shift_right_logical(r, TOP_K_SHIFT) * SUBLANES, SUBLANES),
                 SUBLANES)


def _slot_rows(dest_ref, r):
    return pl.ds(pl.multiple_of(dest_ref[r], SUBLANES), SUBLANES)


def _start_rows(row_copy, n):
    def pair(k, c):
        row_copy(2 * k).start(priority=0)
        row_copy(2 * k + 1).start(priority=1)
        return c

    lax.fori_loop(0, n // 2, pair, 0, unroll=4)


def _wait_rows(row_copy, n):
    def one(r, c):
        row_copy(r).wait()
        return c

    lax.fori_loop(0, n, one, 0, unroll=8)


def _dispatch_kernel(zs_ref, nu_ref, dest_ref, x_ref, xs_ref, zbuf_ref, sem, zsem):
    i = pl.program_id(0)
    n = dest_ref.shape[0]
    blk_rows = MOE_ROWS * SUBLANES
    n_blocks = xs_ref.shape[0] // blk_rows

    def zero_copy(start):
        dst = xs_ref.at[pl.ds(pl.multiple_of(start, blk_rows), blk_rows)]
        return pltpu.make_async_copy(zbuf_ref, dst, zsem)

    @pl.when(i == 0)
    def _():
        zbuf_ref[...] = jnp.zeros_like(zbuf_ref)
        for e in range(N_EXPERTS):
            @pl.when(zs_ref[e] >= 0)
            def _():
                zero_copy(zs_ref[e]).start()

        def tail_start(b, c):
            zero_copy(b * blk_rows).start()
            return c

        def tail_wait(b, c):
            zero_copy(b * blk_rows).wait()
            return c

        lax.fori_loop(nu_ref[0], n_blocks, tail_start, 0)
        for e in range(N_EXPERTS):
            @pl.when(zs_ref[e] >= 0)
            def _():
                zero_copy(zs_ref[e]).wait()
        lax.fori_loop(nu_ref[0], n_blocks, tail_wait, 0)

    def row_copy(r):
        return pltpu.make_async_copy(x_ref.at[_tok_rows(r)], xs_ref.at[_slot_rows(dest_ref, r)],
                                     sem)

    _start_rows(row_copy, n)
    _wait_rows(row_copy, n)


def _dispatch(x1s, dest8, zero_start8, n_used, n_pad):
    tm = DISPATCH_ROWS
    n_tok = x1s.shape[0] // SUBLANES
    return pl.pallas_call(
        _dispatch_kernel,
        grid_spec=pltpu.PrefetchScalarGridSpec(
            num_scalar_prefetch=2,
            grid=(n_tok // tm,),
            in_specs=[pl.BlockSpec((tm * TOP_K,), lambda i, z, u: (i,), memory_space=pltpu.SMEM),
                      pl.BlockSpec((tm * SUBLANES, LANES), lambda i, z, u: (i, 0))],
            out_specs=pl.BlockSpec(memory_space=pl.ANY),
            scratch_shapes=[pltpu.VMEM((MOE_ROWS * SUBLANES, LANES), F32),
                            pltpu.SemaphoreType.DMA, pltpu.SemaphoreType.DMA]),
        out_shape=jax.ShapeDtypeStruct((n_pad * SUBLANES, LANES), F32),
        compiler_params=_params("arbitrary"),
        name="moe_dispatch",
    )(zero_start8, n_used, dest8, x1s)


def _expert_kernel(be_ref, nu_ref, xs_ref, wgu_ref, bg_ref, bl_ref, wd_ref, bd_ref, o_ref,
                   sel_ref, wgl_ref, wdb_ref):
    i = pl.program_id(0)
    _, d, f2 = wgu_ref.shape
    f = f2 // 2
    half = SPLIT_COLS // 2
    active = i < nu_ref[0]

    @pl.when(i == 0)
    def _():
        r = lax.broadcasted_iota(jnp.int32, sel_ref.shape, 0)
        c = lax.broadcasted_iota(jnp.int32, sel_ref.shape, 1)
        src = jnp.where(c < half, 2 * c, 2 * (c - half) + 1)
        sel_ref[...] = (r == src).astype(BF16)

    new_expert = (i == 0) | (be_ref[i] != be_ref[jnp.maximum(i - 1, 0)])

    @pl.when(active & new_expert)
    def _():
        def split_rows(rb, carry):
            rs = pl.ds(pl.multiple_of(rb * WPREP_ROWS, WPREP_ROWS), WPREP_ROWS)
            for j in range(f2 // SPLIT_COLS):
                w = wgu_ref[0, rs, j * SPLIT_COLS:(j + 1) * SPLIT_COLS].astype(BF16)
                y = jnp.dot(w, sel_ref[...], preferred_element_type=F32).astype(BF16)
                wgl_ref[rs, j * half:(j + 1) * half] = y[:, :half]
                wgl_ref[rs, f + j * half:f + (j + 1) * half] = y[:, half:]
            return carry

        def cast_rows(rb, carry):
            rs = pl.ds(pl.multiple_of(rb * WPREP_ROWS, WPREP_ROWS), WPREP_ROWS)
            wdb_ref[rs, :] = wd_ref[0, rs, :].astype(BF16)
            return carry

        lax.fori_loop(0, d // WPREP_ROWS, split_rows, 0)
        lax.fori_loop(0, f // WPREP_ROWS, cast_rows, 0)

    @pl.when(active)
    def _():
        xb = _slab_load(xs_ref, MOE_ROWS, d).astype(BF16)
        h = jnp.dot(xb, wgl_ref[...], preferred_element_type=F32)
        xg = jnp.minimum(h[:, :f] + bg_ref[0], SWIGLU_LIMIT)
        xl = jnp.clip(h[:, f:] + bl_ref[0], -SWIGLU_LIMIT, SWIGLU_LIMIT)
        act = xg * jax.nn.sigmoid(SWIGLU_ALPHA * xg) * (xl + 1.0)
        out = jnp.dot(act.astype(BF16), wdb_ref[...], preferred_element_type=F32) + bd_ref[0]
        _slab_store(o_ref, out)

    @pl.when(i >= nu_ref[0])
    def _():
        o_ref[...] = jnp.zeros_like(o_ref)


def _experts(xs, block_e, n_used, w_gate_up, b_gate_up, w_down, b_down):
    E, D, F2 = w_gate_up.shape
    F = F2 // 2
    tm = MOE_ROWS
    n_blocks = xs.shape[0] // (tm * SUBLANES)
    bg = b_gate_up[:, 0::2].reshape(E, 1, F).astype(F32)
    bl = b_gate_up[:, 1::2].reshape(E, 1, F).astype(F32)
    bd = b_down.reshape(E, 1, D).astype(F32)
    blk = lambda i, be, nu: (jnp.minimum(i, nu[0] - 1), 0)
    wmap = lambda i, be, nu: (be[i], 0, 0)
    return pl.pallas_call(
        _expert_kernel,
        grid_spec=pltpu.PrefetchScalarGridSpec(
            num_scalar_prefetch=2,
            grid=(n_blocks,),
            in_specs=[pl.BlockSpec((tm * SUBLANES, LANES), blk),
                      pl.BlockSpec((1, D, F2), wmap),
                      pl.BlockSpec((1, 1, F), wmap), pl.BlockSpec((1, 1, F), wmap),
                      pl.BlockSpec((1, F, D), wmap), pl.BlockSpec((1, 1, D), wmap)],
            out_specs=pl.BlockSpec((tm * SUBLANES, LANES), lambda i, be, nu: (i, 0)),
            scratch_shapes=[pltpu.VMEM((SPLIT_COLS, SPLIT_COLS), BF16),
                            pltpu.VMEM((D, F2), BF16),
                            pltpu.VMEM((F, D), BF16)]),
        out_shape=jax.ShapeDtypeStruct(xs.shape, F32),
        compiler_params=_params("arbitrary"),
        name="moe_experts",
    )(block_e, n_used, xs, w_gate_up, bg, bl, w_down, bd)


def _combine_kernel(dest_ref, gate_ref, x1_ref, p_ref, wpg_ref, wpp_ref, g_ref, b_ref, os_ref,
                    o_ref, rows_ref, sem):
    n = dest_ref.shape[0]
    tm, d = o_ref.shape

    def row_copy(r):
        return pltpu.make_async_copy(os_ref.at[_slot_rows(dest_ref, r)],
                                     rows_ref.at[r & (TOP_K - 1), _tok_rows(r)], sem)

    _start_rows(row_copy, n)
    pp = jnp.dot(p_ref[...].astype(BF16), wpp_ref[...], preferred_element_type=F32)
    _wait_rows(row_copy, n)

    gates = gate_ref[...]
    moe = gates[:, 0:1] * _slab_load(rows_ref, tm, d, lead=(0,))
    for k in range(1, TOP_K):
        moe = moe + gates[:, k:k + 1] * _slab_load(rows_ref, tm, d, lead=(k,))
    r = DEEPNORM_ALPHA * _slab_load(x1_ref, tm, d) + moe
    gate = jax.nn.sigmoid(jnp.dot(r.astype(BF16), wpg_ref[...], preferred_element_type=F32))
    o_ref[...] = _layer_norm(r + gate * pp, g_ref[...], b_ref[...])


def _combine(dest8, gates_t, x1s, p2, w_ple_gate, w_ple_proj, ln_g, ln_b, out_sorted):
    T, pd = p2.shape
    D = w_ple_gate.shape[0]
    tm = COMBINE_ROWS
    full = lambda shape: pl.BlockSpec(shape, lambda i: (0,) * len(shape))
    return pl.pallas_call(
        _combine_kernel,
        grid=(T // tm,),
        in_specs=[pl.BlockSpec((tm * TOP_K,), lambda i: (i,), memory_space=pltpu.SMEM),
                  pl.BlockSpec((tm, TOP_K), lambda i: (i, 0)),
                  pl.BlockSpec((tm * SUBLANES, LANES), lambda i: (i, 0)),
                  pl.BlockSpec((tm, pd), lambda i: (i, 0)),
                  full((D, D)), full((pd, D)), full((1, D)), full((1, D)),
                  pl.BlockSpec(memory_space=pl.ANY)],
        out_specs=pl.BlockSpec((tm, D), lambda i: (i, 0)),
        out_shape=jax.ShapeDtypeStruct((T, D), F32),
        scratch_shapes=[pltpu.VMEM((TOP_K, tm * SUBLANES, LANES), F32), pltpu.SemaphoreType.DMA],
        compiler_params=_params("arbitrary"),
        name="moe_combine",
    )(dest8, gates_t, x1s, p2, w_ple_gate.astype(BF16), w_ple_proj.astype(BF16),
      ln_g.reshape(1, D).astype(F32), ln_b.reshape(1, D).astype(F32), out_sorted)


def kernel(x, p, w_in, lambda_q1, lambda_k1, lambda_q2, lambda_k2, subln_g, ssm_a_re, ssm_a_im,
           ssm_log_dt, ssm_b_re, ssm_b_im, ssm_c_re, ssm_c_im, ssm_d, w_glu, ssm_norm_g, w_out,
           ln1_g, ln1_b, w_router, b_router, w_gate_up, b_gate_up, w_down, b_down, w_ple_gate,
           w_ple_proj, ln2_g, ln2_b):
    B, L, D = x.shape
    T = B * L
    assert D == SUBLANES * LANES, "token-slab layout holds one token per (8, 128) tile"
    for i in range(DEPTH):
        lambda_init = 0.8 - 0.6 * math.exp(-0.3 * i)
        q, kt, v, u = _in_proj(x, w_in[i])
        lam_vecs = jnp.stack([lambda_q1[i], lambda_k1[i], lambda_q2[i], lambda_k2[i]]).astype(F32)
        attn = _attention(q, kt, v, lam_vecs, subln_g[i], lambda_init)
        b_fold, c_fold, a_tile = _ssm_fold_params(ssm_a_re[i], ssm_a_im[i], ssm_log_dt[i],
                                                  ssm_b_re[i], ssm_b_im[i], ssm_c_re[i],
                                                  ssm_c_im[i], B)
        ssm = _ssm(u, b_fold, c_fold, a_tile, ssm_d[i], w_glu[i], ssm_norm_g[i])

        x1, idx, gates, rank, counts = _mix_route(
            x.reshape(T, D), attn.reshape(T, -1), ssm.reshape(T, -1), w_out[i], ln1_g[i],
            ln1_b[i], w_router[i], b_router[i])

        cnt = counts[:, 0].astype(jnp.int32)
        padded = ((cnt + MOE_ROWS - 1) // MOE_ROWS) * MOE_ROWS
        pad_end = jnp.cumsum(padded)
        pad_start = pad_end - padded
        n_blocks = (T * TOP_K) // MOE_ROWS + N_EXPERTS
        n_pad = n_blocks * MOE_ROWS
        e_ids = jnp.arange(N_EXPERTS, dtype=jnp.int32)[:, None, None]
        start_of = jnp.sum(jnp.where(idx[None] == e_ids, pad_start[:, None, None], 0), axis=0)
        dest8 = ((start_of + rank) * SUBLANES).T.reshape(-1)
        blk_row = jnp.arange(n_blocks, dtype=jnp.int32) * MOE_ROWS
        block_e = jnp.minimum(jnp.sum(pad_end[None, :] <= blk_row[:, None], axis=1),
                              N_EXPERTS - 1).astype(jnp.int32)
        n_used = (pad_end[-1:] // MOE_ROWS).astype(jnp.int32)
        zero_start8 = jnp.where(padded > 0, (pad_end - MOE_ROWS) * SUBLANES, -1).astype(jnp.int32)

        xs = _dispatch(x1, dest8, zero_start8, n_used, n_pad)
        out_sorted = _experts(xs, block_e, n_used, w_gate_up[i], b_gate_up[i], w_down[i],
                              b_down[i])
        x = _combine(dest8, gates.T, x1, p[i].reshape(T, -1), w_ple_gate[i], w_ple_proj[i],
                     ln2_g[i], ln2_b[i], out_sorted).reshape(B, L, D)
    return x
```

```python
import functools
import math

import jax
import jax.numpy as jnp
from jax import lax
from jax.experimental import pallas as pl
from jax.experimental.pallas import tpu as pltpu

F32 = jnp.float32
BF16 = jnp.bfloat16

N_HEADS = 8
HEAD_DIM = 32
V_DIM = 2 * HEAD_DIM
D_ATTN = N_HEADS * V_DIM
SSM_GROUP = 16
SSM_STATE = 64
N_EXPERTS = 32
TOP_K = 4
TOP_K_SHIFT = 2
SWIGLU_LIMIT = 7.0
SWIGLU_ALPHA = 1.702
LN_EPS = 1e-5
RMS_EPS = 1e-5
DEPTH = 1
DEEPNORM_ALPHA = (2.0 * DEPTH) ** 0.25
LOG2E = math.log2(math.e)

LANES = 128
SUBLANES = 8
VMEM_LIMIT = 56 * 1024 * 1024

PROJ_ROWS = 512
ATTN_Q = 512
BIAS_PARTS = 3
SSM_CHUNK = 128
SSM_MM_ROWS = 256
MIX_ROWS = 512
MOE_ROWS = 256
SPLIT_COLS = 512
WPREP_ROWS = 256
IDX_WIN = 2048
IDX_ALIGN = 1024
IDX_SHIFT = 10
ISSUE_UNROLL = 8
COMBINE_ROWS = 256


def _params(*sem):
    return pltpu.CompilerParams(dimension_semantics=sem, vmem_limit_bytes=VMEM_LIMIT)


def _in_proj_kernel(x_ref, wq_ref, wkt_ref, wv_ref, wu_ref, q_ref, kt_ref, v_ref, u_ref):
    xb = x_ref[0].astype(BF16)
    q = jnp.dot(xb, wq_ref[...], preferred_element_type=F32)
    q_ref[0] = (q * (HEAD_DIM ** -0.5 * LOG2E)).astype(BF16)
    kt = lax.dot_general(wkt_ref[...], xb, (((1,), (1,)), ((), ())),
                         preferred_element_type=F32)
    kt_ref[0, 0] = kt.astype(BF16)
    v_ref[0] = jnp.dot(xb, wv_ref[...], preferred_element_type=F32).astype(BF16)
    u_ref[0] = jnp.dot(xb, wu_ref[...], preferred_element_type=F32)


def _in_proj(x, w_in):
    B, L, D = x.shape
    d_ssm = w_in.shape[1] - 3 * D_ATTN
    tm = PROJ_ROWS
    nt = L // tm
    wb = w_in.astype(BF16)
    wq = wb[:, :D_ATTN]
    wkt = wb[:, D_ATTN:2 * D_ATTN].T
    wv = wb[:, 2 * D_ATTN:3 * D_ATTN]
    wu = wb[:, 3 * D_ATTN:]
    full = lambda shape: pl.BlockSpec(shape, lambda b, i: (0,) * len(shape))
    return pl.pallas_call(
        _in_proj_kernel,
        grid=(B, nt),
        in_specs=[pl.BlockSpec((1, tm, D), lambda b, i: (b, i, 0)),
                  full((D, D_ATTN)), full((D_ATTN, D)), full((D, D_ATTN)), full((D, d_ssm))],
        out_specs=[pl.BlockSpec((1, tm, D_ATTN), lambda b, i: (b, i, 0)),
                   pl.BlockSpec((1, 1, D_ATTN, tm), lambda b, i: (b, i, 0, 0)),
                   pl.BlockSpec((1, tm, D_ATTN), lambda b, i: (b, i, 0)),
                   pl.BlockSpec((1, tm, d_ssm), lambda b, i: (b, i, 0))],
        out_shape=[jax.ShapeDtypeStruct((B, L, D_ATTN), BF16),
                   jax.ShapeDtypeStruct((B, nt, D_ATTN, tm), BF16),
                   jax.ShapeDtypeStruct((B, L, D_ATTN), BF16),
                   jax.ShapeDtypeStruct((B, L, d_ssm), F32)],
        compiler_params=_params("parallel", "parallel"),
        name="in_proj",
    )(x, wq, wkt, wv, wu)


def _attn_kernel(slope_ref, lam_ref, q_ref, kt_ref, v_ref, g_ref, o_ref,
                 q4_ref, jr_ref, p_ref, m_ref, l_ref, acc_ref, *, tq, tk, lambda_init):
    hp = pl.program_id(1)
    qi = pl.program_id(2)
    q0 = qi * tq
    n_cb = tk // LANES

    q = q_ref[0]
    lane_q = lax.broadcasted_iota(jnp.int32, q.shape, 1)
    for c in range(4):
        rs = slice(c * tq, (c + 1) * tq)
        lo = (c // 2) * BIAS_PARTS
        q4_ref[rs, :LANES] = jnp.where(lane_q // HEAD_DIM == c, q, jnp.zeros_like(q))
        q4_ref[rs, LANES:] = ((lane_q >= lo) & (lane_q < lo + BIAS_PARTS)).astype(BF16)

    colf = lax.broadcasted_iota(jnp.int32, (1, tk), 1).astype(F32)
    r_i = lax.broadcasted_iota(jnp.int32, (LANES, tk), 0)
    jr = jnp.zeros((LANES, tk), F32)
    for h in range(2):
        rem = (slope_ref[2 * hp + h] * LOG2E) * colf
        for part in range(BIAS_PARTS):
            piece = rem.astype(BF16).astype(F32)
            jr = jnp.where(r_i == h * BIAS_PARTS + part, piece, jr)
            rem = rem - piece
    jr_ref[...] = jr.astype(BF16)

    m_ref[...] = jnp.full(m_ref.shape, -jnp.inf, F32)
    l_ref[...] = jnp.zeros(l_ref.shape, F32)
    acc_ref[...] = jnp.zeros(acc_ref.shape, F32)

    col = lax.broadcasted_iota(jnp.int32, (1, tk), 1)
    row = lax.broadcasted_iota(jnp.int32, (tq, 1), 0)
    zero_row = jnp.zeros((1, LANES), jnp.int32)

    def v_tile(t):
        return v_ref[0, pl.ds(pl.multiple_of(t * tk, tk), tk), :]

    def tile(t, masked):
        kt_aug = jnp.concatenate([kt_ref[0, t], jr_ref[...]], axis=0)
        s_all = jnp.dot(q4_ref[...], kt_aug, preferred_element_type=F32)
        rel = t * tk - q0 + col
        base = (t * tk - q0 + zero_row).astype(F32)
        for c in range(4):
            rs = slice(c * tq, (c + 1) * tq)
            off = (slope_ref[2 * hp + c // 2] * LOG2E) * base
            s = s_all[rs]
            if masked:
                s = jnp.where(rel <= row, s, -jnp.inf)
            blocks = [s[:, cb * LANES:(cb + 1) * LANES] for cb in range(n_cb)]
            part = blocks[0]
            for blk in blocks[1:]:
                part = jnp.maximum(part, blk)
            m_old = m_ref[rs]
            m_new = jnp.maximum(m_old, jnp.max(part, axis=1, keepdims=True) + off)
            m_ref[rs] = m_new
            alpha = jnp.exp2(m_old - m_new)
            shift = m_new - off
            ps = [jnp.exp2(blk - shift) for blk in blocks]
            psum = ps[0]
            for pb in ps[1:]:
                psum = psum + pb
            l_ref[rs] = alpha * l_ref[rs] + psum
            acc_ref[rs] = alpha * acc_ref[rs]
            p_ref[rs] = jnp.concatenate(ps, axis=1).astype(BF16)
        acc_ref[...] += jnp.dot(p_ref[...], v_tile(t), preferred_element_type=F32)

    n_full = q0 // tk

    def body(t, carry):
        tile(t, False)
        return carry

    lax.fori_loop(0, n_full, body, 0)
    tile(n_full, True)
    acc = acc_ref[...]

    lam = lam_ref[...]
    lane = lax.broadcasted_iota(jnp.int32, (1, LANES), 1)
    outs = []
    for j in range(2):
        o = []
        for c in (2 * j, 2 * j + 1):
            rs = slice(c * tq, (c + 1) * tq)
            o.append(acc[rs] / jnp.sum(l_ref[rs], axis=1, keepdims=True))
        d = o[0] - lam * o[1]
        in_head = lane // V_DIM == j
        ms = jnp.sum(jnp.where(in_head, d * d, 0.0), axis=1, keepdims=True) * (1.0 / V_DIM)
        outs.append(d * lax.rsqrt(ms + RMS_EPS))
    out = jnp.where(lane < V_DIM, outs[0], outs[1]) * g_ref[...] * (1.0 - lambda_init)
    o_ref[0] = out.astype(o_ref.dtype)


def _attention(q, kt, v, lam_vecs, subln_g, lambda_init):
    B, L, _ = q.shape
    tq, tk = ATTN_Q, PROJ_ROWS
    nk = L // tk
    slopes = jnp.exp2(-(jnp.arange(N_HEADS, dtype=F32) + 1.0) * (8.0 / N_HEADS))
    lam = (jnp.exp(jnp.sum(lam_vecs[0] * lam_vecs[1])) - jnp.exp(jnp.sum(lam_vecs[2] * lam_vecs[3]))
           + lambda_init).reshape(1, 1).astype(F32)
    g2 = jnp.tile(subln_g.astype(F32), 2).reshape(1, LANES)
    kern = functools.partial(_attn_kernel, tq=tq, tk=tk, lambda_init=lambda_init)
    return pl.pallas_call(
        kern,
        grid_spec=pltpu.PrefetchScalarGridSpec(
            num_scalar_prefetch=1,
            grid=(B, N_HEADS // 2, L // tq),
            in_specs=[pl.BlockSpec((1, 1), lambda b, h, i, s: (0, 0)),
                      pl.BlockSpec((1, tq, LANES), lambda b, h, i, s: (b, i, h)),
                      pl.BlockSpec((1, nk, LANES, tk), lambda b, h, i, s: (b, 0, h, 0)),
                      pl.BlockSpec((1, L, LANES), lambda b, h, i, s: (b, 0, h)),
                      pl.BlockSpec((1, LANES), lambda b, h, i, s: (0, 0))],
            out_specs=pl.BlockSpec((1, tq, LANES), lambda b, h, i, s: (b, i, h)),
            scratch_shapes=[pltpu.VMEM((4 * tq, 2 * LANES), BF16),
                            pltpu.VMEM((LANES, tk), BF16),
                            pltpu.VMEM((4 * tq, tk), BF16),
                            pltpu.VMEM((4 * tq, LANES), F32),
                            pltpu.VMEM((4 * tq, LANES), F32),
                            pltpu.VMEM((4 * tq, LANES), F32)]),
        out_shape=jax.ShapeDtypeStruct((B, L, D_ATTN), BF16),
        compiler_params=_params("parallel", "parallel", "arbitrary"),
        name="diff_attn",
    )(slopes, lam, q, kt, v, g2)


def _ssm_fold_params(a_re, a_im, log_dt, b_re, b_im, c_re, c_im, batch):
    G, P = a_re.shape
    C = b_re.shape[-1]
    half = G // 2
    A = lax.complex(a_re.astype(F32), a_im.astype(F32))
    dt = jnp.exp(log_dt.astype(F32))[:, None]
    a_bar = jnp.exp(A * dt)
    b_bar = ((a_bar - 1.0) / A)[..., None] * lax.complex(b_re.astype(F32), b_im.astype(F32))
    sel = jnp.eye(half, dtype=F32)[jnp.arange(G) % half]
    fold_b = lambda m: jnp.einsum('gpc,gk->gckp', m, sel).reshape(G * C, half * P)
    b_fold = jnp.concatenate([fold_b(jnp.real(b_bar)), fold_b(jnp.imag(b_bar))], axis=1)
    fold_c = lambda m: jnp.einsum('gcp,gk->kpgc', m, sel).reshape(half * P, G * C)
    c_fold = jnp.concatenate([fold_c(c_re.astype(F32)), -fold_c(c_im.astype(F32))], axis=0)
    a_rows = jnp.tile(a_bar.reshape(2, half * P), (batch, 1))
    a_tile = jnp.concatenate([jnp.real(a_rows), jnp.imag(a_rows)], axis=1)
    return b_fold.astype(BF16), c_fold.astype(BF16), a_tile.astype(F32)


def _gelu_tanh(x):
    return 0.5 * x * (1.0 + jnp.tanh(math.sqrt(2.0 / math.pi) * (x + 0.044715 * (x * x * x))))


def _ssm_kernel(u_ref, bf_ref, cf_ref, a_ref, d_ref, wglu_ref, gn_ref, o_ref,
                lhs_ref, x_ref, y_ref, s_ref, *, chunk, batch):
    i = pl.program_id(0)
    d_ssm = u_ref.shape[-1]
    n_slab = d_ssm // LANES
    ns = x_ref.shape[-1] // 2
    rows = SUBLANES

    @pl.when(i == 0)
    def _():
        s_ref[...] = jnp.zeros_like(s_ref)

    zero = jnp.zeros((chunk, LANES), F32)
    for b in range(batch):
        ub = u_ref[b]
        for h in range(2):
            for c in range(n_slab):
                src = ub[:, c * LANES:(c + 1) * LANES] if (c * 2) // n_slab == h else zero
                lhs_ref[c, pl.ds(2 * b + h, chunk, stride=rows), :] = src
    n_blk = (rows * chunk) // SSM_MM_ROWS

    def in_mm(rb, carry):
        rs = pl.ds(pl.multiple_of(rb * SSM_MM_ROWS, SSM_MM_ROWS), SSM_MM_ROWS)
        lhs = jnp.concatenate([lhs_ref[c, rs, :] for c in range(n_slab)], axis=1).astype(BF16)
        x_ref[rs, :] = jnp.dot(lhs, bf_ref[...], preferred_element_type=F32)
        return carry

    lax.fori_loop(0, n_blk, in_mm, 0)

    a_re = a_ref[:, :ns]
    a_im = a_ref[:, ns:]

    def step(t, carry):
        s_re, s_im = carry
        r0 = pl.multiple_of(t * rows, rows)
        n_re = a_re * s_re - a_im * s_im + x_ref[pl.ds(r0, rows), :ns]
        n_im = a_re * s_im + a_im * s_re + x_ref[pl.ds(r0, rows), ns:]
        x_ref[pl.ds(r0, rows), :ns] = n_re
        x_ref[pl.ds(r0, rows), ns:] = n_im
        return n_re, n_im

    s_re, s_im = lax.fori_loop(0, chunk, step, (s_ref[:, :ns], s_ref[:, ns:]), unroll=2)
    s_ref[:, :ns] = s_re
    s_ref[:, ns:] = s_im

    def out_mm(rb, carry):
        rs = pl.ds(pl.multiple_of(rb * SSM_MM_ROWS, SSM_MM_ROWS), SSM_MM_ROWS)
        y = jnp.dot(x_ref[rs, :].astype(BF16), cf_ref[...], preferred_element_type=F32)
        for c in range(n_slab):
            y_ref[c, rs, :] = y[:, c * LANES:(c + 1) * LANES]
        return carry

    lax.fori_loop(0, n_blk, out_mm, 0)

    for b in range(batch):
        parts = []
        for c in range(n_slab):
            h = (c * 2) // n_slab
            parts.append(y_ref[c, pl.ds(2 * b + h, chunk, stride=rows), :])
        yb = jnp.concatenate(parts, axis=1) + d_ref[...] * u_ref[b]
        g = jnp.dot(_gelu_tanh(yb).astype(BF16), wglu_ref[...], preferred_element_type=F32)
        z = g[:, :d_ssm] * jax.nn.sigmoid(g[:, d_ssm:])
        ms = jnp.mean(z * z, axis=1, keepdims=True)
        o_ref[b] = (z * lax.rsqrt(ms + RMS_EPS) * gn_ref[...]).astype(o_ref.dtype)


def _ssm(u, b_fold, c_fold, a_tile, d_skip, w_glu, norm_g):
    B, L, d_ssm = u.shape
    assert 2 * B == SUBLANES, "row-stream layout packs batch x 2 group halves on 8 sublanes"
    chunk = SSM_CHUNK
    ns2 = a_tile.shape[1]
    n_slab = d_ssm // LANES
    full = lambda shape: pl.BlockSpec(shape, lambda i: (0,) * len(shape))
    kern = functools.partial(_ssm_kernel, chunk=chunk, batch=B)
    return pl.pallas_call(
        kern,
        grid=(L // chunk,),
        in_specs=[pl.BlockSpec((B, chunk, d_ssm), lambda i: (0, i, 0)),
                  full((d_ssm, ns2)), full((ns2, d_ssm)), full((SUBLANES, ns2)),
                  full((1, d_ssm)), full((d_ssm, 2 * d_ssm)), full((1, d_ssm))],
        out_specs=pl.BlockSpec((B, chunk, d_ssm), lambda i: (0, i, 0)),
        out_shape=jax.ShapeDtypeStruct((B, L, d_ssm), BF16),
        scratch_shapes=[pltpu.VMEM((n_slab, SUBLANES * chunk, LANES), F32),
                        pltpu.VMEM((SUBLANES * chunk, ns2), F32),
                        pltpu.VMEM((n_slab, SUBLANES * chunk, LANES), F32),
                        pltpu.VMEM((SUBLANES, ns2), F32)],
        compiler_params=_params("arbitrary"),
        name="s5_ssm",
    )(u, b_fold, c_fold, a_tile, d_skip.reshape(1, d_ssm).astype(F32), w_glu.astype(BF16),
      norm_g.reshape(1, d_ssm).astype(F32))


def _layer_norm(x, g, b):
    mu = jnp.mean(x, axis=1, keepdims=True)
    xc = x - mu
    var = jnp.mean(xc * xc, axis=1, keepdims=True)
    return xc * lax.rsqrt(var + LN_EPS) * g + b


def _slab_load(ref, n_rows, d, lead=()):
    return jnp.concatenate(
        [ref[lead + (pl.ds(c, n_rows, stride=SUBLANES), slice(None))] for c in range(d // LANES)],
        axis=1)


def _slab_store(ref, val):
    n_rows, d = val.shape
    for c in range(d // LANES):
        ref[pl.ds(c, n_rows, stride=SUBLANES), :] = val[:, c * LANES:(c + 1) * LANES]


def _mix_kernel(x_ref, attn_ref, ssm_ref, woa_ref, wos_ref, g_ref, b_ref, wrt_ref, br_ref,
                x1_ref, idx_ref, gate_ref, cnt_ref, carry_ref):
    i = pl.program_id(0)
    tm = x_ref.shape[0]

    @pl.when(i == 0)
    def _():
        carry_ref[...] = jnp.zeros_like(carry_ref)

    mix = (jnp.dot(attn_ref[...], woa_ref[...], preferred_element_type=F32)
           + jnp.dot(ssm_ref[...], wos_ref[...], preferred_element_type=F32))
    x1 = _layer_norm(DEEPNORM_ALPHA * x_ref[...] + mix, g_ref[...], b_ref[...])
    _slab_store(x1_ref, x1)

    logits = lax.dot_general(wrt_ref[...], x1.astype(BF16), (((1,), (1,)), ((), ())),
                             preferred_element_type=F32) + br_ref[...]
    e_iota = lax.broadcasted_iota(jnp.int32, logits.shape, 0).astype(F32)
    work = logits
    vals, hots = [], []
    for k in range(TOP_K):
        mx = jnp.max(work, axis=0, keepdims=True)
        idx = jnp.min(jnp.where(work == mx, e_iota, float(N_EXPERTS)), axis=0, keepdims=True)
        hot = e_iota == idx
        work = jnp.where(hot, -jnp.inf, work)
        vals.append(mx)
        hots.append(hot)
        idx_ref[k:k + 1, :] = idx.astype(jnp.int32)
    ex = [jnp.exp(v - vals[0]) for v in vals]
    den = ex[0] + ex[1] + ex[2] + ex[3]
    for k in range(TOP_K):
        gate_ref[k:k + 1, :] = ex[k] / den

    sel = (hots[0] | hots[1] | hots[2] | hots[3]).astype(F32)
    carry_ref[...] = carry_ref[...] + jnp.sum(sel, axis=1, keepdims=True)
    cnt_ref[...] = carry_ref[...]


def _mix_route(x2, attn2, ssm2, w_out, ln_g, ln_b, w_router, b_router):
    T, D = x2.shape
    tm = MIX_ROWS
    d_attn = attn2.shape[1]
    wo = w_out.astype(BF16)
    full = lambda shape: pl.BlockSpec(shape, lambda i: (0,) * len(shape))
    rowblk = lambda w: pl.BlockSpec((tm, w), lambda i: (i, 0))
    colblk = pl.BlockSpec((TOP_K, tm), lambda i: (0, i))
    return pl.pallas_call(
        _mix_kernel,
        grid=(T // tm,),
        in_specs=[rowblk(D), rowblk(d_attn), rowblk(ssm2.shape[1]),
                  full((d_attn, D)), full((D - d_attn, D)), full((1, D)), full((1, D)),
                  full((N_EXPERTS, D)), full((N_EXPERTS, 1))],
        out_specs=[pl.BlockSpec((tm * SUBLANES, LANES), lambda i: (i, 0)),
                   colblk, colblk, full((N_EXPERTS, LANES))],
        out_shape=[jax.ShapeDtypeStruct((T * SUBLANES, LANES), F32),
                   jax.ShapeDtypeStruct((TOP_K, T), jnp.int32),
                   jax.ShapeDtypeStruct((TOP_K, T), F32),
                   jax.ShapeDtypeStruct((N_EXPERTS, LANES), F32)],
        scratch_shapes=[pltpu.VMEM((N_EXPERTS, LANES), F32)],
        compiler_params=_params("arbitrary"),
        name="mix_route",
    )(x2, attn2, ssm2, wo[:d_attn], wo[d_attn:], ln_g.reshape(1, D).astype(F32),
      ln_b.reshape(1, D).astype(F32), w_router.T.astype(BF16),
      b_router.reshape(N_EXPERTS, 1).astype(F32))


def _expert_kernel(be_ref, nu_ref, src_ref, cnt_ref, gsrc_ref, sdst_ref, x1_ref, wgu_ref, bg_ref,
                   bl_ref, wd_ref, bd_ref, o4_ref, sel_ref, wgl_ref, wdb_ref, xbuf, obuf, ibuf,
                   gsem, ssem, isem):
    i = pl.program_id(0)
    nu = nu_ref[0]
    _, d, f2 = wgu_ref.shape
    f = f2 // 2
    half = SPLIT_COLS // 2
    active = i < nu
    blk_rows = MOE_ROWS * SUBLANES

    def idx_copies(j):
        base = pl.multiple_of(lax.shift_right_logical(src_ref[j], IDX_SHIFT) * IDX_ALIGN, IDX_ALIGN)
        b = lax.rem(j, 3)
        dst = lambda h: ibuf.at[pl.ds(pl.multiple_of((2 * b + h) * IDX_WIN, IDX_WIN), IDX_WIN)]
        return (pltpu.make_async_copy(gsrc_ref.at[pl.ds(base, IDX_WIN)], dst(0), isem.at[b]),
                pltpu.make_async_copy(sdst_ref.at[pl.ds(base, IDX_WIN)], dst(1), isem.at[b]))

    def entry_base(j, h):
        return (2 * lax.rem(j, 3) + h) * IDX_WIN + (src_ref[j] & (IDX_ALIGN - 1))

    def rows(start, n=SUBLANES):
        return pl.ds(pl.multiple_of(start, SUBLANES), n)

    def gather_row(j, ebase, r):
        return pltpu.make_async_copy(x1_ref.at[rows(ibuf[ebase + r])],
                                     xbuf.at[rows((j & 1) * blk_rows + r * SUBLANES)],
                                     gsem.at[j & 1])

    def scatter_row(j, ebase, r):
        return pltpu.make_async_copy(obuf.at[rows((j & 1) * blk_rows + r * SUBLANES)],
                                     o4_ref.at[rows(ibuf[ebase + r])], ssem.at[j & 1])

    def start_block(row_copy):
        def group(g, c):
            for u in range(ISSUE_UNROLL):
                row_copy(g * ISSUE_UNROLL + u).start(priority=u % 2)
            return c

        lax.fori_loop(0, MOE_ROWS // ISSUE_UNROLL, group, 0)

    def start_gather(j):
        ebase = entry_base(j, 0)
        start_block(lambda r: gather_row(j, ebase, r))

    def wait_gather(j):
        pltpu.make_async_copy(x1_ref.at[rows(0, blk_rows)],
                              xbuf.at[rows((j & 1) * blk_rows, blk_rows)], gsem.at[j & 1]).wait()

    def start_scatter(j):
        ebase = entry_base(j, 1)

        @pl.when(cnt_ref[j] == MOE_ROWS)
        def _():
            start_block(lambda r: scatter_row(j, ebase, r))

        @pl.when(cnt_ref[j] < MOE_ROWS)
        def _():
            def one(r, c):
                scatter_row(j, ebase, r).start()
                return c

            lax.fori_loop(0, cnt_ref[j], one, 0)

    def wait_scatter(j):
        @pl.when(cnt_ref[j] == MOE_ROWS)
        def _():
            pltpu.make_async_copy(obuf.at[rows((j & 1) * blk_rows, blk_rows)],
                                  o4_ref.at[rows(0, blk_rows)], ssem.at[j & 1]).wait()

        @pl.when(cnt_ref[j] < MOE_ROWS)
        def _():
            def one(r, c):
                pltpu.make_async_copy(obuf.at[rows(0)], o4_ref.at[rows(0)], ssem.at[j & 1]).wait()
                return c

            lax.fori_loop(0, cnt_ref[j], one, 0)

    @pl.when(i == 0)
    def _():
        r = lax.broadcasted_iota(jnp.int32, sel_ref.shape, 0)
        c = lax.broadcasted_iota(jnp.int32, sel_ref.shape, 1)
        src = jnp.where(c < half, 2 * c, 2 * (c - half) + 1)
        sel_ref[...] = (r == src).astype(BF16)
        for cp in idx_copies(0):
            cp.start()
        for cp in idx_copies(0):
            cp.wait()

        @pl.when(nu > 1)
        def _():
            for cp in idx_copies(1):
                cp.start()

        start_gather(0)

    @pl.when(active)
    def _():
        @pl.when(i + 2 < nu)
        def _():
            for cp in idx_copies(i + 2):
                cp.start()

        wait_gather(i)

        @pl.when(i + 1 < nu)
        def _():
            for cp in idx_copies(i + 1):
                cp.wait()
            start_gather(i + 1)

    new_expert = (i == 0) | (be_ref[i] != be_ref[jnp.maximum(i - 1, 0)])

    @pl.when(active & new_expert)
    def _():
        def split_rows(rb, carry):
            rs = pl.ds(pl.multiple_of(rb * WPREP_ROWS, WPREP_ROWS), WPREP_ROWS)
            for j in range(f2 // SPLIT_COLS):
                w = wgu_ref[0, rs, j * SPLIT_COLS:(j + 1) * SPLIT_COLS].astype(BF16)
                y = jnp.dot(w, sel_ref[...], preferred_element_type=F32).astype(BF16)
                wgl_ref[rs, j * half:(j + 1) * half] = y[:, :half]
                wgl_ref[rs, f + j * half:f + (j + 1) * half] = y[:, half:]
            return carry

        def cast_rows(rb, carry):
            rs = pl.ds(pl.multiple_of(rb * WPREP_ROWS, WPREP_ROWS), WPREP_ROWS)
            wdb_ref[rs, :] = wd_ref[0, rs, :].astype(BF16)
            return carry

        lax.fori_loop(0, d // WPREP_ROWS, split_rows, 0)
        lax.fori_loop(0, f // WPREP_ROWS, cast_rows, 0)

    @pl.when(active)
    def _():
        base = pl.multiple_of((i & 1) * blk_rows, blk_rows)
        xb = _slab_load(xbuf.at[pl.ds(base, blk_rows)], MOE_ROWS, d).astype(BF16)
        h = jnp.dot(xb, wgl_ref[...], preferred_element_type=F32)
        xg = jnp.minimum(h[:, :f] + bg_ref[0], SWIGLU_LIMIT)
        xl = jnp.clip(h[:, f:] + bl_ref[0], -SWIGLU_LIMIT, SWIGLU_LIMIT)
        act = xg * jax.nn.sigmoid(SWIGLU_ALPHA * xg) * (xl + 1.0)
        out = jnp.dot(act.astype(BF16), wdb_ref[...], preferred_element_type=F32) + bd_ref[0]
        _slab_store(obuf.at[pl.ds(base, blk_rows)], out)

        start_scatter(i)

        @pl.when(i >= 1)
        def _():
            wait_scatter(i - 1)

        @pl.when(i == nu - 1)
        def _():
            wait_scatter(i)


def _experts(x1s, gsrc, sdst, block_e, n_used, blk_src, blk_cnt, w_gate_up, b_gate_up, w_down,
             b_down):
    E, D, F2 = w_gate_up.shape
    F = F2 // 2
    tm = MOE_ROWS
    n_blocks = block_e.shape[0]
    bg = b_gate_up[:, 0::2].reshape(E, 1, F).astype(F32)
    bl = b_gate_up[:, 1::2].reshape(E, 1, F).astype(F32)
    bd = b_down.reshape(E, 1, D).astype(F32)
    wmap = lambda i, be, nu, sr, cn: (be[i], 0, 0)
    any_spec = pl.BlockSpec(memory_space=pl.ANY)
    return pl.pallas_call(
        _expert_kernel,
        grid_spec=pltpu.PrefetchScalarGridSpec(
            num_scalar_prefetch=4,
            grid=(n_blocks,),
            in_specs=[any_spec, any_spec, any_spec,
                      pl.BlockSpec((1, D, F2), wmap),
                      pl.BlockSpec((1, 1, F), wmap), pl.BlockSpec((1, 1, F), wmap),
                      pl.BlockSpec((1, F, D), wmap), pl.BlockSpec((1, 1, D), wmap)],
            out_specs=any_spec,
            scratch_shapes=[pltpu.VMEM((SPLIT_COLS, SPLIT_COLS), BF16),
                            pltpu.VMEM((D, F2), BF16),
                            pltpu.VMEM((F, D), BF16),
                            pltpu.VMEM((2 * tm * SUBLANES, LANES), F32),
                            pltpu.VMEM((2 * tm * SUBLANES, LANES), F32),
                            pltpu.SMEM((3 * 2 * IDX_WIN,), jnp.int32),
                            pltpu.SemaphoreType.DMA((2,)), pltpu.SemaphoreType.DMA((2,)),
                            pltpu.SemaphoreType.DMA((3,))]),
        out_shape=jax.ShapeDtypeStruct((TOP_K * x1s.shape[0], LANES), F32),
        compiler_params=_params("arbitrary"),
        name="moe_experts",
    )(block_e, n_used, blk_src, blk_cnt, gsrc, sdst, x1s, w_gate_up, bg, bl, w_down, bd)


def _combine_kernel(gate_ref, x1_ref, o4_ref, p_ref, wpg_ref, wpp_ref, g_ref, b_ref, o_ref):
    tm, d = o_ref.shape
    pp = jnp.dot(p_ref[...].astype(BF16), wpp_ref[...], preferred_element_type=F32)
    gates = gate_ref[...]
    moe = gates[:, 0:1] * _slab_load(o4_ref, tm, d, lead=(0,))
    for k in range(1, TOP_K):
        moe = moe + gates[:, k:k + 1] * _slab_load(o4_ref, tm, d, lead=(k,))
    r = DEEPNORM_ALPHA * _slab_load(x1_ref, tm, d) + moe
    gate = jax.nn.sigmoid(jnp.dot(r.astype(BF16), wpg_ref[...], preferred_element_type=F32))
    o_ref[...] = _layer_norm(r + gate * pp, g_ref[...], b_ref[...])


def _combine(gates_t, x1s, out4, p2, w_ple_gate, w_ple_proj, ln_g, ln_b):
    T, pd = p2.shape
    D = w_ple_gate.shape[0]
    tm = COMBINE_ROWS
    full = lambda shape: pl.BlockSpec(shape, lambda i: (0,) * len(shape))
    return pl.pallas_call(
        _combine_kernel,
        grid=(T // tm,),
        in_specs=[pl.BlockSpec((tm, TOP_K), lambda i: (i, 0)),
                  pl.BlockSpec((tm * SUBLANES, LANES), lambda i: (i, 0)),
                  pl.BlockSpec((TOP_K, tm * SUBLANES, LANES), lambda i: (0, i, 0)),
                  pl.BlockSpec((tm, pd), lambda i: (i, 0)),
                  full((D, D)), full((pd, D)), full((1, D)), full((1, D))],
        out_specs=pl.BlockSpec((tm, D), lambda i: (i, 0)),
        out_shape=jax.ShapeDtypeStruct((T, D), F32),
        compiler_params=_params("parallel"),
        name="moe_combine",
    )(gates_t, x1s, out4, p2, w_ple_gate.astype(BF16), w_ple_proj.astype(BF16),
      ln_g.reshape(1, D).astype(F32), ln_b.reshape(1, D).astype(F32))


def kernel(x, p, w_in, lambda_q1, lambda_k1, lambda_q2, lambda_k2, subln_g, ssm_a_re, ssm_a_im,
           ssm_log_dt, ssm_b_re, ssm_b_im, ssm_c_re, ssm_c_im, ssm_d, w_glu, ssm_norm_g, w_out,
           ln1_g, ln1_b, w_router, b_router, w_gate_up, b_gate_up, w_down, b_down, w_ple_gate,
           w_ple_proj, ln2_g, ln2_b):
    B, L, D = x.shape
    T = B * L
    assert D == SUBLANES * LANES, "token-slab layout holds one token per (8, 128) tile"
    for i in range(DEPTH):
        lambda_init = 0.8 - 0.6 * math.exp(-0.3 * i)
        q, kt, v, u = _in_proj(x, w_in[i])
        lam_vecs = jnp.stack([lambda_q1[i], lambda_k1[i], lambda_q2[i], lambda_k2[i]]).astype(F32)
        attn = _attention(q, kt, v, lam_vecs, subln_g[i], lambda_init)
        b_fold, c_fold, a_tile = _ssm_fold_params(ssm_a_re[i], ssm_a_im[i], ssm_log_dt[i],
                                                  ssm_b_re[i], ssm_b_im[i], ssm_c_re[i],
                                                  ssm_c_im[i], B)
        ssm = _ssm(u, b_fold, c_fold, a_tile, ssm_d[i], w_glu[i], ssm_norm_g[i])

        x1, idx, gates, counts = _mix_route(
            x.reshape(T, D), attn.reshape(T, -1), ssm.reshape(T, -1), w_out[i], ln1_g[i],
            ln1_b[i], w_router[i], b_router[i])

        order = jnp.argsort(idx.T.reshape(-1), stable=True).astype(jnp.int32)
        order = jnp.concatenate([order, jnp.zeros((IDX_WIN,), jnp.int32)])
        tok_row = lax.shift_right_logical(order, TOP_K_SHIFT) * SUBLANES
        gsrc = tok_row
        sdst = (order & (TOP_K - 1)) * (T * SUBLANES) + tok_row
        cnt = counts[:, 0].astype(jnp.int32)
        start = jnp.cumsum(cnt) - cnt
        n_blk_e = (cnt + MOE_ROWS - 1) // MOE_ROWS
        blk_end = jnp.cumsum(n_blk_e)
        n_blocks = (T * TOP_K) // MOE_ROWS + N_EXPERTS
        blk = jnp.arange(n_blocks, dtype=jnp.int32)
        block_e = jnp.minimum(jnp.sum(blk_end[None, :] <= blk[:, None], axis=1),
                              N_EXPERTS - 1).astype(jnp.int32)
        local = blk - (blk_end - n_blk_e)[block_e]
        blk_src = (start[block_e] + local * MOE_ROWS).astype(jnp.int32)
        blk_cnt = jnp.clip(cnt[block_e] - local * MOE_ROWS, 0, MOE_ROWS).astype(jnp.int32)
        n_used = blk_end[-1:].astype(jnp.int32)
        blk_src = jnp.where(blk < n_used, blk_src, 0)
        blk_cnt = jnp.where(blk < n_used, blk_cnt, 0)

        out4 = _experts(x1, gsrc, sdst, block_e, n_used, blk_src, blk_cnt, w_gate_up[i],
                        b_gate_up[i], w_down[i], b_down[i])
        x = _combine(gates.T, x1, out4.reshape(TOP_K, T * SUBLANES, LANES), p[i].reshape(T, -1), w_ple_gate[i], w_ple_proj[i],
                     ln2_g[i], ln2_b[i]).reshape(B, L, D)
    return x
```

```python
import functools
import math

import jax
import jax.numpy as jnp
from jax import lax
from jax.experimental import pallas as pl
from jax.experimental.pallas import tpu as pltpu

F32 = jnp.float32
BF16 = jnp.bfloat16

N_HEADS = 8
HEAD_DIM = 32
V_DIM = 2 * HEAD_DIM
D_ATTN = N_HEADS * V_DIM
SSM_GROUP = 16
SSM_STATE = 64
N_EXPERTS = 32
TOP_K = 4
TOP_K_SHIFT = 2
SWIGLU_LIMIT = 7.0
SWIGLU_ALPHA = 1.702
LN_EPS = 1e-5
RMS_EPS = 1e-5
DEPTH = 1
DEEPNORM_ALPHA = (2.0 * DEPTH) ** 0.25
LOG2E = math.log2(math.e)

LANES = 128
SUBLANES = 8
VMEM_LIMIT = 56 * 1024 * 1024

PROJ_ROWS = 512
ATTN_Q = 512
BIAS_PARTS = 3
SSM_CHUNK = 128
SSM_MM_ROWS = 256
MIX_ROWS = 512
MOE_ROWS = 256
SPLIT_COLS = 512
WPREP_ROWS = 256
IDX_WIN = 2048
IDX_ALIGN = 1024
IDX_SHIFT = 10
COMBINE_ROWS = 256


def _params(*sem):
    return pltpu.CompilerParams(dimension_semantics=sem, vmem_limit_bytes=VMEM_LIMIT)


def _in_proj_kernel(x_ref, wq_ref, wkt_ref, wv_ref, wu_ref, q_ref, kt_ref, v_ref, u_ref):
    xb = x_ref[0].astype(BF16)
    q = jnp.dot(xb, wq_ref[...], preferred_element_type=F32)
    q_ref[0] = (q * (HEAD_DIM ** -0.5 * LOG2E)).astype(BF16)
    kt = lax.dot_general(wkt_ref[...], xb, (((1,), (1,)), ((), ())),
                         preferred_element_type=F32)
    kt_ref[0, 0] = kt.astype(BF16)
    v_ref[0] = jnp.dot(xb, wv_ref[...], preferred_element_type=F32).astype(BF16)
    u_ref[0] = jnp.dot(xb, wu_ref[...], preferred_element_type=F32)


def _in_proj(x, w_in):
    B, L, D = x.shape
    d_ssm = w_in.shape[1] - 3 * D_ATTN
    tm = PROJ_ROWS
    nt = L // tm
    wb = w_in.astype(BF16)
    wq = wb[:, :D_ATTN]
    wkt = wb[:, D_ATTN:2 * D_ATTN].T
    wv = wb[:, 2 * D_ATTN:3 * D_ATTN]
    wu = wb[:, 3 * D_ATTN:]
    full = lambda shape: pl.BlockSpec(shape, lambda b, i: (0,) * len(shape))
    return pl.pallas_call(
        _in_proj_kernel,
        grid=(B, nt),
        in_specs=[pl.BlockSpec((1, tm, D), lambda b, i: (b, i, 0)),
                  full((D, D_ATTN)), full((D_ATTN, D)), full((D, D_ATTN)), full((D, d_ssm))],
        out_specs=[pl.BlockSpec((1, tm, D_ATTN), lambda b, i: (b, i, 0)),
                   pl.BlockSpec((1, 1, D_ATTN, tm), lambda b, i: (b, i, 0, 0)),
                   pl.BlockSpec((1, tm, D_ATTN), lambda b, i: (b, i, 0)),
                   pl.BlockSpec((1, tm, d_ssm), lambda b, i: (b, i, 0))],
        out_shape=[jax.ShapeDtypeStruct((B, L, D_ATTN), BF16),
                   jax.ShapeDtypeStruct((B, nt, D_ATTN, tm), BF16),
                   jax.ShapeDtypeStruct((B, L, D_ATTN), BF16),
                   jax.ShapeDtypeStruct((B, L, d_ssm), F32)],
        compiler_params=_params("parallel", "parallel"),
        name="in_proj",
    )(x, wq, wkt, wv, wu)


def _attn_kernel(slope_ref, lam_ref, q_ref, kt_ref, v_ref, g_ref, o_ref,
                 q4_ref, jr_ref, p_ref, m_ref, l_ref, acc_ref, *, tq, tk, lambda_init):
    hp = pl.program_id(1)
    qi = pl.program_id(2)
    q0 = qi * tq
    n_cb = tk // LANES

    q = q_ref[0]
    lane_q = lax.broadcasted_iota(jnp.int32, q.shape, 1)
    for c in range(4):
        rs = slice(c * tq, (c + 1) * tq)
        lo = (c // 2) * BIAS_PARTS
        q4_ref[rs, :LANES] = jnp.where(lane_q // HEAD_DIM == c, q, jnp.zeros_like(q))
        q4_ref[rs, LANES:] = ((lane_q >= lo) & (lane_q < lo + BIAS_PARTS)).astype(BF16)

    @pl.when(qi == 0)
    def _():
        colf = lax.broadcasted_iota(jnp.int32, (1, tk), 1).astype(F32)
        r_i = lax.broadcasted_iota(jnp.int32, (LANES, tk), 0)
        jr = jnp.zeros((LANES, tk), F32)
        for h in range(2):
            rem = (slope_ref[2 * hp + h] * LOG2E) * colf
            for part in range(BIAS_PARTS):
                piece = rem.astype(BF16).astype(F32)
                jr = jnp.where(r_i == h * BIAS_PARTS + part, piece, jr)
                rem = rem - piece
        jr_ref[...] = jr.astype(BF16)

    m_ref[...] = jnp.full(m_ref.shape, -jnp.inf, F32)
    l_ref[...] = jnp.zeros(l_ref.shape, F32)
    acc_ref[...] = jnp.zeros(acc_ref.shape, F32)

    col = lax.broadcasted_iota(jnp.int32, (1, tk), 1)
    row = lax.broadcasted_iota(jnp.int32, (tq, 1), 0)
    zero_row = jnp.zeros((1, LANES), jnp.int32)

    def v_tile(t):
        return v_ref[0, pl.ds(pl.multiple_of(t * tk, tk), tk), :]

    def tile(t, masked):
        kt_aug = jnp.concatenate([kt_ref[0, t], jr_ref[...]], axis=0)
        s_all = jnp.dot(q4_ref[...], kt_aug, preferred_element_type=F32)
        rel = t * tk - q0 + col
        base = (t * tk - q0 + zero_row).astype(F32)
        for c in range(4):
            rs = slice(c * tq, (c + 1) * tq)
            off = (slope_ref[2 * hp + c // 2] * LOG2E) * base
            s = s_all[rs]
            if masked:
                s = jnp.where(rel <= row, s, -jnp.inf)
            blocks = [s[:, cb * LANES:(cb + 1) * LANES] for cb in range(n_cb)]
            part = blocks[0]
            for blk in blocks[1:]:
                part = jnp.maximum(part, blk)
            m_old = m_ref[rs]
            m_new = jnp.maximum(m_old, jnp.max(part, axis=1, keepdims=True) + off)
            m_ref[rs] = m_new
            alpha = jnp.exp2(m_old - m_new)
            shift = m_new - off
            ps = [jnp.exp2(blk - shift) for blk in blocks]
            psum = ps[0]
            for pb in ps[1:]:
                psum = psum + pb
            l_ref[rs] = alpha * l_ref[rs] + psum
            acc_ref[rs] = alpha * acc_ref[rs]
            p_ref[rs] = jnp.concatenate(ps, axis=1).astype(BF16)
        acc_ref[...] += jnp.dot(p_ref[...], v_tile(t), preferred_element_type=F32)

    n_full = q0 // tk

    def body(t, carry):
        tile(t, False)
        return carry

    lax.fori_loop(0, n_full, body, 0)
    tile(n_full, True)
    acc = acc_ref[...]

    lam = lam_ref[...]
    lane = lax.broadcasted_iota(jnp.int32, (1, LANES), 1)
    outs = []
    for j in range(2):
        o = []
        for c in (2 * j, 2 * j + 1):
            rs = slice(c * tq, (c + 1) * tq)
            o.append(acc[rs] / jnp.sum(l_ref[rs], axis=1, keepdims=True))
        d = o[0] - lam * o[1]
        in_head = lane // V_DIM == j
        ms = jnp.sum(jnp.where(in_head, d * d, 0.0), axis=1, keepdims=True) * (1.0 / V_DIM)
        outs.append(d * lax.rsqrt(ms + RMS_EPS))
    out = jnp.where(lane < V_DIM, outs[0], outs[1]) * g_ref[...] * (1.0 - lambda_init)
    o_ref[0] = out.astype(o_ref.dtype)


def _attention(q, kt, v, lam_vecs, subln_g, lambda_init):
    B, L, _ = q.shape
    tq, tk = ATTN_Q, PROJ_ROWS
    nk = L // tk
    slopes = jnp.exp2(-(jnp.arange(N_HEADS, dtype=F32) + 1.0) * (8.0 / N_HEADS))
    lam = (jnp.exp(jnp.sum(lam_vecs[0] * lam_vecs[1])) - jnp.exp(jnp.sum(lam_vecs[2] * lam_vecs[3]))
           + lambda_init).reshape(1, 1).astype(F32)
    g2 = jnp.tile(subln_g.astype(F32), 2).reshape(1, LANES)
    kern = functools.partial(_attn_kernel, tq=tq, tk=tk, lambda_init=lambda_init)
    return pl.pallas_call(
        kern,
        grid_spec=pltpu.PrefetchScalarGridSpec(
            num_scalar_prefetch=1,
            grid=(B, N_HEADS // 2, L // tq),
            in_specs=[pl.BlockSpec((1, 1), lambda b, h, i, s: (0, 0)),
                      pl.BlockSpec((1, tq, LANES), lambda b, h, i, s: (b, i, h)),
                      pl.BlockSpec((1, nk, LANES, tk), lambda b, h, i, s: (b, 0, h, 0)),
                      pl.BlockSpec((1, L, LANES), lambda b, h, i, s: (b, 0, h)),
                      pl.BlockSpec((1, LANES), lambda b, h, i, s: (0, 0))],
            out_specs=pl.BlockSpec((1, tq, LANES), lambda b, h, i, s: (b, i, h)),
            scratch_shapes=[pltpu.VMEM((4 * tq, 2 * LANES), BF16),
                            pltpu.VMEM((LANES, tk), BF16),
                            pltpu.VMEM((4 * tq, tk), BF16),
                            pltpu.VMEM((4 * tq, LANES), F32),
                            pltpu.VMEM((4 * tq, LANES), F32),
                            pltpu.VMEM((4 * tq, LANES), F32)]),
        out_shape=jax.ShapeDtypeStruct((B, L, D_ATTN), BF16),
        compiler_params=_params("parallel", "parallel", "arbitrary"),
        name="diff_attn",
    )(slopes, lam, q, kt, v, g2)


def _ssm_fold_params(a_re, a_im, log_dt, b_re, b_im, c_re, c_im, batch):
    G, P = a_re.shape
    C = b_re.shape[-1]
    half = G // 2
    A = lax.complex(a_re.astype(F32), a_im.astype(F32))
    dt = jnp.exp(log_dt.astype(F32))[:, None]
    a_bar = jnp.exp(A * dt)
    b_bar = ((a_bar - 1.0) / A)[..., None] * lax.complex(b_re.astype(F32), b_im.astype(F32))
    sel = jnp.eye(half, dtype=F32)[jnp.arange(G) % half]
    fold_b = lambda m: jnp.einsum('gpc,gk->gckp', m, sel).reshape(G * C, half * P)
    b_fold = jnp.concatenate([fold_b(jnp.real(b_bar)), fold_b(jnp.imag(b_bar))], axis=1)
    fold_c = lambda m: jnp.einsum('gcp,gk->kpgc', m, sel).reshape(half * P, G * C)
    c_fold = jnp.concatenate([fold_c(c_re.astype(F32)), -fold_c(c_im.astype(F32))], axis=0)
    a_rows = jnp.tile(a_bar.reshape(2, half * P), (batch, 1))
    a_tile = jnp.concatenate([jnp.real(a_rows), jnp.imag(a_rows)], axis=1)
    return b_fold.astype(BF16), c_fold.astype(BF16), a_tile.astype(F32)


def _gelu_tanh(x):
    return 0.5 * x * (1.0 + jnp.tanh(math.sqrt(2.0 / math.pi) * (x + 0.044715 * (x * x * x))))


def _ssm_kernel(u_ref, bf_ref, cf_ref, a_ref, d_ref, wglu_ref, gn_ref, o_ref,
                lhs_ref, x_ref, y_ref, s_ref, *, chunk, batch):
    i = pl.program_id(0)
    d_ssm = u_ref.shape[-1]
    n_slab = d_ssm // LANES
    ns = x_ref.shape[-1] // 2
    rows = SUBLANES

    @pl.when(i == 0)
    def _():
        s_ref[...] = jnp.zeros_like(s_ref)

    zero = jnp.zeros((chunk, LANES), F32)
    for b in range(batch):
        ub = u_ref[b]
        for h in range(2):
            for c in range(n_slab):
                src = ub[:, c * LANES:(c + 1) * LANES] if (c * 2) // n_slab == h else zero
                lhs_ref[c, pl.ds(2 * b + h, chunk, stride=rows), :] = src
    n_blk = (rows * chunk) // SSM_MM_ROWS

    hs = ns // 2
    n_set = 2

    def in_mm(rb, carry):
        rs = pl.ds(pl.multiple_of(rb * SSM_MM_ROWS, SSM_MM_ROWS), SSM_MM_ROWS)
        for s in range(n_set):
            lhs = jnp.concatenate([lhs_ref[s, rs, :], lhs_ref[s + n_slab // 2, rs, :]],
                                  axis=1).astype(BF16)
            xs = jnp.dot(lhs, bf_ref[s], preferred_element_type=F32)
            x_ref[rs, s * hs:(s + 1) * hs] = xs[:, :hs]
            x_ref[rs, ns + s * hs:ns + (s + 1) * hs] = xs[:, hs:]
        return carry

    lax.fori_loop(0, n_blk, in_mm, 0)

    a_re = a_ref[:, :ns]
    a_im = a_ref[:, ns:]

    def step(t, carry):
        s_re, s_im = carry
        r0 = pl.multiple_of(t * rows, rows)
        n_re = a_re * s_re - a_im * s_im + x_ref[pl.ds(r0, rows), :ns]
        n_im = a_re * s_im + a_im * s_re + x_ref[pl.ds(r0, rows), ns:]
        x_ref[pl.ds(r0, rows), :ns] = n_re
        x_ref[pl.ds(r0, rows), ns:] = n_im
        return n_re, n_im

    s_re, s_im = lax.fori_loop(0, chunk, step, (s_ref[:, :ns], s_ref[:, ns:]), unroll=2)
    s_ref[:, :ns] = s_re
    s_ref[:, ns:] = s_im

    def out_mm(rb, carry):
        rs = pl.ds(pl.multiple_of(rb * SSM_MM_ROWS, SSM_MM_ROWS), SSM_MM_ROWS)
        for s in range(n_set):
            st = jnp.concatenate([x_ref[rs, s * hs:(s + 1) * hs],
                                  x_ref[rs, ns + s * hs:ns + (s + 1) * hs]], axis=1).astype(BF16)
            y = jnp.dot(st, cf_ref[s], preferred_element_type=F32)
            y_ref[s, rs, :] = y[:, :LANES]
            y_ref[s + n_slab // 2, rs, :] = y[:, LANES:]
        return carry

    lax.fori_loop(0, n_blk, out_mm, 0)

    for b in range(batch):
        parts = []
        for c in range(n_slab):
            h = (c * 2) // n_slab
            parts.append(y_ref[c, pl.ds(2 * b + h, chunk, stride=rows), :])
        yb = jnp.concatenate(parts, axis=1) + d_ref[...] * u_ref[b]
        g = jnp.dot(_gelu_tanh(yb).astype(BF16), wglu_ref[...], preferred_element_type=F32)
        z = g[:, :d_ssm] * jax.nn.sigmoid(g[:, d_ssm:])
        ms = jnp.mean(z * z, axis=1, keepdims=True)
        o_ref[b] = (z * lax.rsqrt(ms + RMS_EPS) * gn_ref[...]).astype(o_ref.dtype)


def _ssm(u, b_fold, c_fold, a_tile, d_skip, w_glu, norm_g):
    B, L, d_ssm = u.shape
    assert 2 * B == SUBLANES, "row-stream layout packs batch x 2 group halves on 8 sublanes"
    chunk = SSM_CHUNK
    ns2 = a_tile.shape[1]
    n_slab = d_ssm // LANES
    assert n_slab == 4, "two group sets x two group halves of 128 channels"
    ns, hs = ns2 // 2, ns2 // 4
    ch = lambda s: jnp.r_[s * LANES:(s + 1) * LANES, (s + 2) * LANES:(s + 3) * LANES]
    st = lambda s: jnp.r_[s * hs:(s + 1) * hs, ns + s * hs:ns + (s + 1) * hs]
    b_sets = jnp.stack([b_fold[ch(s)][:, st(s)] for s in range(2)])
    c_sets = jnp.stack([c_fold[st(s)][:, ch(s)] for s in range(2)])
    full = lambda shape: pl.BlockSpec(shape, lambda i: (0,) * len(shape))
    kern = functools.partial(_ssm_kernel, chunk=chunk, batch=B)
    return pl.pallas_call(
        kern,
        grid=(L // chunk,),
        in_specs=[pl.BlockSpec((B, chunk, d_ssm), lambda i: (0, i, 0)),
                  full(b_sets.shape), full(c_sets.shape), full((SUBLANES, ns2)),
                  full((1, d_ssm)), full((d_ssm, 2 * d_ssm)), full((1, d_ssm))],
        out_specs=pl.BlockSpec((B, chunk, d_ssm), lambda i: (0, i, 0)),
        out_shape=jax.ShapeDtypeStruct((B, L, d_ssm), BF16),
        scratch_shapes=[pltpu.VMEM((n_slab, SUBLANES * chunk, LANES), F32),
                        pltpu.VMEM((SUBLANES * chunk, ns2), F32),
                        pltpu.VMEM((n_slab, SUBLANES * chunk, LANES), F32),
                        pltpu.VMEM((SUBLANES, ns2), F32)],
        compiler_params=_params("arbitrary"),
        name="s5_ssm",
    )(u, b_sets, c_sets, a_tile, d_skip.reshape(1, d_ssm).astype(F32), w_glu.astype(BF16),
      norm_g.reshape(1, d_ssm).astype(F32))


def _layer_norm(x, g, b):
    mu = jnp.mean(x, axis=1, keepdims=True)
    xc = x - mu
    var = jnp.mean(xc * xc, axis=1, keepdims=True)
    return xc * lax.rsqrt(var + LN_EPS) * g + b


def _slab_load(ref, n_rows, d, lead=()):
    return jnp.concatenate(
        [ref[lead + (pl.ds(c, n_rows, stride=SUBLANES), slice(None))] for c in range(d // LANES)],
        axis=1)


def _slab_store(ref, val):
    n_rows, d = val.shape
    for c in range(d // LANES):
        ref[pl.ds(c, n_rows, stride=SUBLANES), :] = val[:, c * LANES:(c + 1) * LANES]


def _mix_kernel(x_ref, attn_ref, ssm_ref, woa_ref, wos_ref, g_ref, b_ref, wrt_ref, br_ref,
                x1_ref, idx_ref, gate_ref, cnt_ref, carry_ref):
    i = pl.program_id(0)
    tm = x_ref.shape[0]

    @pl.when(i == 0)
    def _():
        carry_ref[...] = jnp.zeros_like(carry_ref)

    mix = (jnp.dot(attn_ref[...], woa_ref[...], preferred_element_type=F32)
           + jnp.dot(ssm_ref[...], wos_ref[...], preferred_element_type=F32))
    x1 = _layer_norm(DEEPNORM_ALPHA * x_ref[...] + mix, g_ref[...], b_ref[...])
    _slab_store(x1_ref, x1)

    logits = lax.dot_general(wrt_ref[...], x1.astype(BF16), (((1,), (1,)), ((), ())),
                             preferred_element_type=F32) + br_ref[...]
    e_iota = lax.broadcasted_iota(jnp.int32, logits.shape, 0).astype(F32)
    work = logits
    vals, hots = [], []
    for k in range(TOP_K):
        mx = jnp.max(work, axis=0, keepdims=True)
        idx = jnp.min(jnp.where(work == mx, e_iota, float(N_EXPERTS)), axis=0, keepdims=True)
        hot = e_iota == idx
        work = jnp.where(hot, -jnp.inf, work)
        vals.append(mx)
        hots.append(hot)
        idx_ref[k:k + 1, :] = idx.astype(jnp.int32)
    ex = [jnp.exp(v - vals[0]) for v in vals]
    den = ex[0] + ex[1] + ex[2] + ex[3]
    for k in range(TOP_K):
        gate_ref[k:k + 1, :] = ex[k] / den

    sel = (hots[0] | hots[1] | hots[2] | hots[3]).astype(F32)
    carry_ref[...] = carry_ref[...] + jnp.sum(sel, axis=1, keepdims=True)
    cnt_ref[...] = carry_ref[...]


def _mix_route(x2, attn2, ssm2, w_out, ln_g, ln_b, w_router, b_router):
    T, D = x2.shape
    tm = MIX_ROWS
    d_attn = attn2.shape[1]
    wo = w_out.astype(BF16)
    full = lambda shape: pl.BlockSpec(shape, lambda i: (0,) * len(shape))
    rowblk = lambda w: pl.BlockSpec((tm, w), lambda i: (i, 0))
    colblk = pl.BlockSpec((TOP_K, tm), lambda i: (0, i))
    return pl.pallas_call(
        _mix_kernel,
        grid=(T // tm,),
        in_specs=[rowblk(D), rowblk(d_attn), rowblk(ssm2.shape[1]),
                  full((d_attn, D)), full((D - d_attn, D)), full((1, D)), full((1, D)),
                  full((N_EXPERTS, D)), full((N_EXPERTS, 1))],
        out_specs=[pl.BlockSpec((tm * SUBLANES, LANES), lambda i: (i, 0)),
                   colblk, colblk, full((N_EXPERTS, LANES))],
        out_shape=[jax.ShapeDtypeStruct((T * SUBLANES, LANES), F32),
                   jax.ShapeDtypeStruct((TOP_K, T), jnp.int32),
                   jax.ShapeDtypeStruct((TOP_K, T), F32),
                   jax.ShapeDtypeStruct((N_EXPERTS, LANES), F32)],
        scratch_shapes=[pltpu.VMEM((N_EXPERTS, LANES), F32)],
        compiler_params=_params("arbitrary"),
        name="mix_route",
    )(x2, attn2, ssm2, wo[:d_attn], wo[d_attn:], ln_g.reshape(1, D).astype(F32),
      ln_b.reshape(1, D).astype(F32), w_router.T.astype(BF16),
      b_router.reshape(N_EXPERTS, 1).astype(F32))


def _expert_kernel(be_ref, nu_ref, src_ref, cnt_ref, gsrc_ref, sdst_ref, x1_ref, wgu_ref, bg_ref,
                   bl_ref, wd_ref, bd_ref, o4_ref, sel_ref, wgl_ref, wdb_ref, xbuf, obuf, ibuf,
                   gsem, ssem, isem):
    i = pl.program_id(0)
    nu = nu_ref[0]
    _, d, f2 = wgu_ref.shape
    f = f2 // 2
    half = SPLIT_COLS // 2
    active = i < nu
    blk_rows = MOE_ROWS * SUBLANES

    def idx_copies(j):
        base = pl.multiple_of(lax.shift_right_logical(src_ref[j], IDX_SHIFT) * IDX_ALIGN, IDX_ALIGN)
        b = lax.rem(j, 3)
        dst = lambda h: ibuf.at[pl.ds(pl.multiple_of((2 * b + h) * IDX_WIN, IDX_WIN), IDX_WIN)]
        return (pltpu.make_async_copy(gsrc_ref.at[pl.ds(base, IDX_WIN)], dst(0), isem.at[b]),
                pltpu.make_async_copy(sdst_ref.at[pl.ds(base, IDX_WIN)], dst(1), isem.at[b]))

    def entry_base(j, h):
        return (2 * lax.rem(j, 3) + h) * IDX_WIN + (src_ref[j] & (IDX_ALIGN - 1))

    def rows(start, n=SUBLANES):
        if isinstance(start, int):
            return pl.ds(start, n)
        return pl.ds(pl.multiple_of(start, SUBLANES), n)

    def gather_row(slot, ebase, r):
        return pltpu.make_async_copy(x1_ref.at[rows(ibuf[ebase + r])],
                                     xbuf.at[rows(slot * blk_rows + r * SUBLANES)],
                                     gsem.at[slot])

    def scatter_row(slot, ebase, r):
        return pltpu.make_async_copy(obuf.at[rows(slot * blk_rows + r * SUBLANES)],
                                     o4_ref.at[rows(ibuf[ebase + r])], ssem.at[slot])

    def start_block(j, row_copy):
        for slot in range(2):
            @pl.when((j & 1) == slot)
            def _():
                for r in range(MOE_ROWS):
                    row_copy(slot, r).start(priority=r % 2)

    def start_gather(j):
        ebase = entry_base(j, 0)
        start_block(j, lambda slot, r: gather_row(slot, ebase, r))

    def wait_gather(j):
        pltpu.make_async_copy(x1_ref.at[rows(0, blk_rows)],
                              xbuf.at[rows((j & 1) * blk_rows, blk_rows)], gsem.at[j & 1]).wait()

    def start_scatter(j):
        ebase = entry_base(j, 1)

        @pl.when(cnt_ref[j] == MOE_ROWS)
        def _():
            start_block(j, lambda slot, r: scatter_row(slot, ebase, r))

        @pl.when(cnt_ref[j] < MOE_ROWS)
        def _():
            def one(r, c):
                scatter_row(j & 1, ebase, r).start()
                return c

            lax.fori_loop(0, cnt_ref[j], one, 0)

    def wait_scatter(j):
        @pl.when(cnt_ref[j] == MOE_ROWS)
        def _():
            pltpu.make_async_copy(obuf.at[rows((j & 1) * blk_rows, blk_rows)],
                                  o4_ref.at[rows(0, blk_rows)], ssem.at[j & 1]).wait()

        @pl.when(cnt_ref[j] < MOE_ROWS)
        def _():
            def one(r, c):
                pltpu.make_async_copy(obuf.at[rows(0)], o4_ref.at[rows(0)], ssem.at[j & 1]).wait()
                return c

            lax.fori_loop(0, cnt_ref[j], one, 0)

    @pl.when(i == 0)
    def _():
        r = lax.broadcasted_iota(jnp.int32, sel_ref.shape, 0)
        c = lax.broadcasted_iota(jnp.int32, sel_ref.shape, 1)
        src = jnp.where(c < half, 2 * c, 2 * (c - half) + 1)
        sel_ref[...] = (r == src).astype(BF16)
        for cp in idx_copies(0):
            cp.start()
        for cp in idx_copies(0):
            cp.wait()

        @pl.when(nu > 1)
        def _():
            for cp in idx_copies(1):
                cp.start()

        ebase0 = entry_base(0, 0)

        def first(r, c):
            gather_row(0, ebase0, r).start()
            return c

        lax.fori_loop(0, MOE_ROWS, first, 0)

    @pl.when(active)
    def _():
        @pl.when(i + 2 < nu)
        def _():
            for cp in idx_copies(i + 2):
                cp.start()

        wait_gather(i)

        @pl.when(i + 1 < nu)
        def _():
            for cp in idx_copies(i + 1):
                cp.wait()
            start_gather(i + 1)

    new_expert = (i == 0) | (be_ref[i] != be_ref[jnp.maximum(i - 1, 0)])

    @pl.when(active & new_expert)
    def _():
        def split_rows(rb, carry):
            rs = pl.ds(pl.multiple_of(rb * WPREP_ROWS, WPREP_ROWS), WPREP_ROWS)
            for j in range(f2 // SPLIT_COLS):
                w = wgu_ref[0, rs, j * SPLIT_COLS:(j + 1) * SPLIT_COLS].astype(BF16)
                y = jnp.dot(w, sel_ref[...], preferred_element_type=F32).astype(BF16)
                wgl_ref[rs, j * half:(j + 1) * half] = y[:, :half]
                wgl_ref[rs, f + j * half:f + (j + 1) * half] = y[:, half:]
            return carry

        def cast_rows(rb, carry):
            rs = pl.ds(pl.multiple_of(rb * WPREP_ROWS, WPREP_ROWS), WPREP_ROWS)
            wdb_ref[rs, :] = wd_ref[0, rs, :].astype(BF16)
            return carry

        lax.fori_loop(0, d // WPREP_ROWS, split_rows, 0)
        lax.fori_loop(0, f // WPREP_ROWS, cast_rows, 0)

    @pl.when(active)
    def _():
        base = pl.multiple_of((i & 1) * blk_rows, blk_rows)
        xb = _slab_load(xbuf.at[pl.ds(base, blk_rows)], MOE_ROWS, d).astype(BF16)
        h = jnp.dot(xb, wgl_ref[...], preferred_element_type=F32)
        xg = jnp.minimum(h[:, :f] + bg_ref[0], SWIGLU_LIMIT)
        xl = jnp.clip(h[:, f:] + bl_ref[0], -SWIGLU_LIMIT, SWIGLU_LIMIT)
        act = xg * jax.nn.sigmoid(SWIGLU_ALPHA * xg) * (xl + 1.0)
        out = jnp.dot(act.astype(BF16), wdb_ref[...], preferred_element_type=F32) + bd_ref[0]
        _slab_store(obuf.at[pl.ds(base, blk_rows)], out)

        start_scatter(i)

        @pl.when(i >= 1)
        def _():
            wait_scatter(i - 1)

        @pl.when(i == nu - 1)
        def _():
            wait_scatter(i)


def _experts(x1s, gsrc, sdst, block_e, n_used, blk_src, blk_cnt, w_gate_up, b_gate_up, w_down,
             b_down):
    E, D, F2 = w_gate_up.shape
    F = F2 // 2
    tm = MOE_ROWS
    n_blocks = block_e.shape[0]
    bg = b_gate_up[:, 0::2].reshape(E, 1, F).astype(F32)
    bl = b_gate_up[:, 1::2].reshape(E, 1, F).astype(F32)
    bd = b_down.reshape(E, 1, D).astype(F32)
    wmap = lambda i, be, nu, sr, cn: (be[i], 0, 0)
    any_spec = pl.BlockSpec(memory_space=pl.ANY)
    return pl.pallas_call(
        _expert_kernel,
        grid_spec=pltpu.PrefetchScalarGridSpec(
            num_scalar_prefetch=4,
            grid=(n_blocks,),
            in_specs=[any_spec, any_spec, any_spec,
                      pl.BlockSpec((1, D, F2), wmap),
                      pl.BlockSpec((1, 1, F), wmap), pl.BlockSpec((1, 1, F), wmap),
                      pl.BlockSpec((1, F, D), wmap), pl.BlockSpec((1, 1, D), wmap)],
            out_specs=any_spec,
            scratch_shapes=[pltpu.VMEM((SPLIT_COLS, SPLIT_COLS), BF16),
                            pltpu.VMEM((D, F2), BF16),
                            pltpu.VMEM((F, D), BF16),
                            pltpu.VMEM((2 * tm * SUBLANES, LANES), F32),
                            pltpu.VMEM((2 * tm * SUBLANES, LANES), F32),
                            pltpu.SMEM((3 * 2 * IDX_WIN,), jnp.int32),
                            pltpu.SemaphoreType.DMA((2,)), pltpu.SemaphoreType.DMA((2,)),
                            pltpu.SemaphoreType.DMA((3,))]),
        out_shape=jax.ShapeDtypeStruct((TOP_K * x1s.shape[0], LANES), F32),
        compiler_params=_params("arbitrary"),
        name="moe_experts",
    )(block_e, n_used, blk_src, blk_cnt, gsrc, sdst, x1s, w_gate_up, bg, bl, w_down, bd)


def _combine_kernel(gate_ref, x1_ref, o4_ref, p_ref, wpg_ref, wpp_ref, g_ref, b_ref, o_ref):
    tm, d = o_ref.shape
    pp = jnp.dot(p_ref[...].astype(BF16), wpp_ref[...], preferred_element_type=F32)
    gates = gate_ref[...]
    moe = gates[:, 0:1] * _slab_load(o4_ref, tm, d, lead=(0,))
    for k in range(1, TOP_K):
        moe = moe + gates[:, k:k + 1] * _slab_load(o4_ref, tm, d, lead=(k,))
    r = DEEPNORM_ALPHA * _slab_load(x1_ref, tm, d) + moe
    gate = jax.nn.sigmoid(jnp.dot(r.astype(BF16), wpg_ref[...], preferred_element_type=F32))
    o_ref[...] = _layer_norm(r + gate * pp, g_ref[...], b_ref[...])


def _combine(gates_t, x1s, out4, p2, w_ple_gate, w_ple_proj, ln_g, ln_b):
    T, pd = p2.shape
    D = w_ple_gate.shape[0]
    tm = COMBINE_ROWS
    full = lambda shape: pl.BlockSpec(shape, lambda i: (0,) * len(shape))
    return pl.pallas_call(
        _combine_kernel,
        grid=(T // tm,),
        in_specs=[pl.BlockSpec((tm, TOP_K), lambda i: (i, 0)),
                  pl.BlockSpec((tm * SUBLANES, LANES), lambda i: (i, 0)),
                  pl.BlockSpec((TOP_K, tm * SUBLANES, LANES), lambda i: (0, i, 0)),
                  pl.BlockSpec((tm, pd), lambda i: (i, 0)),
                  full((D, D)), full((pd, D)), full((1, D)), full((1, D))],
        out_specs=pl.BlockSpec((tm, D), lambda i: (i, 0)),
        out_shape=jax.ShapeDtypeStruct((T, D), F32),
        compiler_params=_params("parallel"),
        name="moe_combine",
    )(gates_t, x1s, out4, p2, w_ple_gate.astype(BF16), w_ple_proj.astype(BF16),
      ln_g.reshape(1, D).astype(F32), ln_b.reshape(1, D).astype(F32))


def kernel(x, p, w_in, lambda_q1, lambda_k1, lambda_q2, lambda_k2, subln_g, ssm_a_re, ssm_a_im,
           ssm_log_dt, ssm_b_re, ssm_b_im, ssm_c_re, ssm_c_im, ssm_d, w_glu, ssm_norm_g, w_out,
           ln1_g, ln1_b, w_router, b_router, w_gate_up, b_gate_up, w_down, b_down, w_ple_gate,
           w_ple_proj, ln2_g, ln2_b):
    B, L, D = x.shape
    T = B * L
    assert D == SUBLANES * LANES, "token-slab layout holds one token per (8, 128) tile"
    for i in range(DEPTH):
        lambda_init = 0.8 - 0.6 * math.exp(-0.3 * i)
        q, kt, v, u = _in_proj(x, w_in[i])
        lam_vecs = jnp.stack([lambda_q1[i], lambda_k1[i], lambda_q2[i], lambda_k2[i]]).astype(F32)
        attn = _attention(q, kt, v, lam_vecs, subln_g[i], lambda_init)
        b_fold, c_fold, a_tile = _ssm_fold_params(ssm_a_re[i], ssm_a_im[i], ssm_log_dt[i],
                                                  ssm_b_re[i], ssm_b_im[i], ssm_c_re[i],
                                                  ssm_c_im[i], B)
        ssm = _ssm(u, b_fold, c_fold, a_tile, ssm_d[i], w_glu[i], ssm_norm_g[i])

        x1, idx, gates, counts = _mix_route(
            x.reshape(T, D), attn.reshape(T, -1), ssm.reshape(T, -1), w_out[i], ln1_g[i],
            ln1_b[i], w_router[i], b_router[i])

        order = jnp.argsort(idx.T.reshape(-1), stable=True).astype(jnp.int32)
        order = jnp.concatenate([order, jnp.zeros((IDX_WIN,), jnp.int32)])
        tok_row = lax.shift_right_logical(order, TOP_K_SHIFT) * SUBLANES
        gsrc = tok_row
        sdst = (order & (TOP_K - 1)) * (T * SUBLANES) + tok_row
        cnt = counts[:, 0].astype(jnp.int32)
        start = jnp.cumsum(cnt) - cnt
        n_blk_e = (cnt + MOE_ROWS - 1) // MOE_ROWS
        blk_end = jnp.cumsum(n_blk_e)
        n_blocks = (T * TOP_K) // MOE_ROWS + N_EXPERTS
        blk = jnp.arange(n_blocks, dtype=jnp.int32)
        block_e = jnp.minimum(jnp.sum(blk_end[None, :] <= blk[:, None], axis=1),
                              N_EXPERTS - 1).astype(jnp.int32)
        local = blk - (blk_end - n_blk_e)[block_e]
        blk_src = (start[block_e] + local * MOE_ROWS).astype(jnp.int32)
        blk_cnt = jnp.clip(cnt[block_e] - local * MOE_ROWS, 0, MOE_ROWS).astype(jnp.int32)
        n_used = blk_end[-1:].astype(jnp.int32)
        blk_src = jnp.where(blk < n_used, blk_src, 0)
        blk_cnt = jnp.where(blk < n_used, blk_cnt, 0)

        out4 = _experts(x1, gsrc, sdst, block_e, n_used, blk_src, blk_cnt, w_gate_up[i],
                        b_gate_up[i], w_down[i], b_down[i])
        x = _combine(gates.T, x1, out4.reshape(TOP_K, T * SUBLANES, LANES), p[i].reshape(T, -1), w_ple_gate[i], w_ple_proj[i],
                     ln2_g[i], ln2_b[i]).reshape(B, L, D)
    return x
```

```python
import functools
import math

import jax
import jax.numpy as jnp
from jax import lax
from jax.experimental import pallas as pl
from jax.experimental.pallas import tpu as pltpu

F32 = jnp.float32
BF16 = jnp.bfloat16

N_HEADS = 8
HEAD_DIM = 32
V_DIM = 2 * HEAD_DIM
D_ATTN = N_HEADS * V_DIM
SSM_GROUP = 16
SSM_STATE = 64
N_EXPERTS = 32
TOP_K = 4
TOP_K_SHIFT = 2
SWIGLU_LIMIT = 7.0
SWIGLU_ALPHA = 1.702
LN_EPS = 1e-5
RMS_EPS = 1e-5
DEPTH = 1
DEEPNORM_ALPHA = (2.0 * DEPTH) ** 0.25
LOG2E = math.log2(math.e)

LANES = 128
SUBLANES = 8
VMEM_LIMIT = 56 * 1024 * 1024

PROJ_ROWS = 512
ATTN_Q = 512
BIAS_PARTS = 3
SSM_CHUNK = 128
SSM_MM_ROWS = 256
MIX_ROWS = 512
MOE_ROWS = 256
SPLIT_COLS = 512
WPREP_ROWS = 256
IDX_WIN = 2048
IDX_ALIGN = 1024
IDX_SHIFT = 10
COMBINE_ROWS = 256


def _params(*sem):
    return pltpu.CompilerParams(dimension_semantics=sem, vmem_limit_bytes=VMEM_LIMIT)


def _in_proj_kernel(x_ref, wq_ref, wkt_ref, wv_ref, wu_ref, q_ref, kt_ref, v_ref, u_ref):
    xb = x_ref[0].astype(BF16)
    q = jnp.dot(xb, wq_ref[...], preferred_element_type=F32)
    q_ref[0] = (q * (HEAD_DIM ** -0.5 * LOG2E)).astype(BF16)
    kt = lax.dot_general(wkt_ref[...], xb, (((1,), (1,)), ((), ())),
                         preferred_element_type=F32)
    kt_ref[0, 0] = kt.astype(BF16)
    lane_v = lax.broadcasted_iota(jnp.int32, (1, wv_ref.shape[1]), 1)
    ones_lane = (lane_v % LANES == V_DIM).astype(F32)
    v_ref[0] = (jnp.dot(xb, wv_ref[...], preferred_element_type=F32) + ones_lane).astype(BF16)
    u_ref[0] = jnp.dot(xb, wu_ref[...], preferred_element_type=F32)


def _in_proj(x, w_in):
    B, L, D = x.shape
    d_ssm = w_in.shape[1] - 3 * D_ATTN
    tm = PROJ_ROWS
    nt = L // tm
    wb = w_in.astype(BF16)
    wq = wb[:, :D_ATTN]
    wkt = wb[:, D_ATTN:2 * D_ATTN].T
    wv = wb[:, 2 * D_ATTN:3 * D_ATTN].reshape(D, N_HEADS, V_DIM)
    wv = jnp.pad(wv, ((0, 0), (0, 0), (0, LANES - V_DIM))).reshape(D, N_HEADS * LANES)
    wu = wb[:, 3 * D_ATTN:]
    full = lambda shape: pl.BlockSpec(shape, lambda b, i: (0,) * len(shape))
    return pl.pallas_call(
        _in_proj_kernel,
        grid=(B, nt),
        in_specs=[pl.BlockSpec((1, tm, D), lambda b, i: (b, i, 0)),
                  full((D, D_ATTN)), full((D_ATTN, D)), full((D, N_HEADS * LANES)),
                  full((D, d_ssm))],
        out_specs=[pl.BlockSpec((1, tm, D_ATTN), lambda b, i: (b, i, 0)),
                   pl.BlockSpec((1, 1, D_ATTN, tm), lambda b, i: (b, i, 0, 0)),
                   pl.BlockSpec((1, tm, N_HEADS * LANES), lambda b, i: (b, i, 0)),
                   pl.BlockSpec((1, tm, d_ssm), lambda b, i: (b, i, 0))],
        out_shape=[jax.ShapeDtypeStruct((B, L, D_ATTN), BF16),
                   jax.ShapeDtypeStruct((B, nt, D_ATTN, tm), BF16),
                   jax.ShapeDtypeStruct((B, L, N_HEADS * LANES), BF16),
                   jax.ShapeDtypeStruct((B, L, d_ssm), F32)],
        compiler_params=_params("parallel", "parallel"),
        name="in_proj",
    )(x, wq, wkt, wv, wu)


def _attn_kernel(slope_ref, lam_ref, q_ref, kt_ref, v_ref, g_ref, o_ref,
                 q4_ref, jr_ref, p_ref, m_ref, acc_ref, *, tq, tk, lambda_init):
    hp = pl.program_id(1)
    qi = pl.program_id(2)
    q0 = qi * tq
    n_cb = tk // LANES

    q = q_ref[0]
    lane_q = lax.broadcasted_iota(jnp.int32, q.shape, 1)
    for c in range(4):
        rs = slice(c * tq, (c + 1) * tq)
        lo = (c // 2) * BIAS_PARTS
        q4_ref[rs, :LANES] = jnp.where(lane_q // HEAD_DIM == c, q, jnp.zeros_like(q))
        q4_ref[rs, LANES:] = ((lane_q >= lo) & (lane_q < lo + BIAS_PARTS)).astype(BF16)

    @pl.when(qi == 0)
    def _():
        colf = lax.broadcasted_iota(jnp.int32, (1, tk), 1).astype(F32)
        r_i = lax.broadcasted_iota(jnp.int32, (LANES, tk), 0)
        jr = jnp.zeros((LANES, tk), F32)
        for h in range(2):
            rem = (slope_ref[2 * hp + h] * LOG2E) * colf
            for part in range(BIAS_PARTS):
                piece = rem.astype(BF16).astype(F32)
                jr = jnp.where(r_i == h * BIAS_PARTS + part, piece, jr)
                rem = rem - piece
        jr_ref[...] = jr.astype(BF16)

    m_ref[...] = jnp.full(m_ref.shape, -jnp.inf, F32)
    acc_ref[...] = jnp.zeros(acc_ref.shape, F32)

    col = lax.broadcasted_iota(jnp.int32, (1, tk), 1)
    row = lax.broadcasted_iota(jnp.int32, (tq, 1), 0)
    zero_row = jnp.zeros((1, LANES), jnp.int32)

    def v_tile(t):
        return v_ref[0, pl.ds(pl.multiple_of(t * tk, tk), tk), :]

    def tile(t, masked):
        kt_aug = jnp.concatenate([kt_ref[0, t], jr_ref[...]], axis=0)
        s_all = jnp.dot(q4_ref[...], kt_aug, preferred_element_type=F32)
        rel = t * tk - q0 + col
        base = (t * tk - q0 + zero_row).astype(F32)
        for c in range(4):
            rs = slice(c * tq, (c + 1) * tq)
            off = (slope_ref[2 * hp + c // 2] * LOG2E) * base
            s = s_all[rs]
            if masked:
                s = jnp.where(rel <= row, s, -jnp.inf)
            blocks = [s[:, cb * LANES:(cb + 1) * LANES] for cb in range(n_cb)]
            part = blocks[0]
            for blk in blocks[1:]:
                part = jnp.maximum(part, blk)
            m_old = m_ref[rs]
            m_new = jnp.maximum(m_old, jnp.max(part, axis=1, keepdims=True) + off)
            m_ref[rs] = m_new
            alpha = jnp.exp2(m_old - m_new)
            shift = m_new - off
            acc_ref[rs] = alpha * acc_ref[rs]
            p_ref[rs] = jnp.concatenate([jnp.exp2(blk - shift) for blk in blocks],
                                        axis=1).astype(BF16)
        v2 = v_tile(t)
        for j in range(2):
            hr = slice(2 * j * tq, (2 * j + 2) * tq)
            acc_ref[hr] += jnp.dot(p_ref[hr], v2[:, j * LANES:(j + 1) * LANES],
                                   preferred_element_type=F32)

    n_full = q0 // tk

    def body(t, carry):
        tile(t, False)
        return carry

    lax.fori_loop(0, n_full, body, 0)
    tile(n_full, True)
    acc = acc_ref[...]

    lam = lam_ref[...]
    lane = lax.broadcasted_iota(jnp.int32, (1, LANES), 1)
    outs = []
    for j in range(2):
        o = []
        for c in (2 * j, 2 * j + 1):
            a = acc[c * tq:(c + 1) * tq]
            o.append(a / a[:, V_DIM:V_DIM + 1])
        d = o[0] - lam * o[1]
        ms = jnp.sum(jnp.where(lane < V_DIM, d * d, 0.0), axis=1, keepdims=True) * (1.0 / V_DIM)
        outs.append(d * lax.rsqrt(ms + RMS_EPS))
    out = jnp.where(lane < V_DIM, outs[0], pltpu.roll(outs[1], V_DIM, 1))
    out = out * g_ref[...] * (1.0 - lambda_init)
    o_ref[0] = out.astype(o_ref.dtype)


def _attention(q, kt, v, lam_vecs, subln_g, lambda_init):
    B, L, _ = q.shape
    tq, tk = ATTN_Q, PROJ_ROWS
    nk = L // tk
    slopes = jnp.exp2(-(jnp.arange(N_HEADS, dtype=F32) + 1.0) * (8.0 / N_HEADS))
    lam = (jnp.exp(jnp.sum(lam_vecs[0] * lam_vecs[1])) - jnp.exp(jnp.sum(lam_vecs[2] * lam_vecs[3]))
           + lambda_init).reshape(1, 1).astype(F32)
    g2 = jnp.tile(subln_g.astype(F32), 2).reshape(1, LANES)
    kern = functools.partial(_attn_kernel, tq=tq, tk=tk, lambda_init=lambda_init)
    return pl.pallas_call(
        kern,
        grid_spec=pltpu.PrefetchScalarGridSpec(
            num_scalar_prefetch=1,
            grid=(B, N_HEADS // 2, L // tq),
            in_specs=[pl.BlockSpec((1, 1), lambda b, h, i, s: (0, 0)),
                      pl.BlockSpec((1, tq, LANES), lambda b, h, i, s: (b, i, h)),
                      pl.BlockSpec((1, nk, LANES, tk), lambda b, h, i, s: (b, 0, h, 0)),
                      pl.BlockSpec((1, L, 2 * LANES), lambda b, h, i, s: (b, 0, h)),
                      pl.BlockSpec((1, LANES), lambda b, h, i, s: (0, 0))],
            out_specs=pl.BlockSpec((1, tq, LANES), lambda b, h, i, s: (b, i, h)),
            scratch_shapes=[pltpu.VMEM((4 * tq, 2 * LANES), BF16),
                            pltpu.VMEM((LANES, tk), BF16),
                            pltpu.VMEM((4 * tq, tk), BF16),
                            pltpu.VMEM((4 * tq, LANES), F32),
                            pltpu.VMEM((4 * tq, LANES), F32)]),
        out_shape=jax.ShapeDtypeStruct((B, L, D_ATTN), BF16),
        compiler_params=_params("parallel", "parallel", "arbitrary"),
        name="diff_attn",
    )(slopes, lam, q, kt, v, g2)


def _ssm_fold_params(a_re, a_im, log_dt, b_re, b_im, c_re, c_im, batch):
    G, P = a_re.shape
    C = b_re.shape[-1]
    half = G // 2
    A = lax.complex(a_re.astype(F32), a_im.astype(F32))
    dt = jnp.exp(log_dt.astype(F32))[:, None]
    a_bar = jnp.exp(A * dt)
    b_bar = ((a_bar - 1.0) / A)[..., None] * lax.complex(b_re.astype(F32), b_im.astype(F32))
    sel = jnp.eye(half, dtype=F32)[jnp.arange(G) % half]
    fold_b = lambda m: jnp.einsum('gpc,gk->gckp', m, sel).reshape(G * C, half * P)
    b_fold = jnp.concatenate([fold_b(jnp.real(b_bar)), fold_b(jnp.imag(b_bar))], axis=1)
    fold_c = lambda m: jnp.einsum('gcp,gk->kpgc', m, sel).reshape(half * P, G * C)
    c_fold = jnp.concatenate([fold_c(c_re.astype(F32)), -fold_c(c_im.astype(F32))], axis=0)
    a_rows = jnp.tile(a_bar.reshape(2, half * P), (batch, 1))
    a_tile = jnp.concatenate([jnp.real(a_rows), jnp.imag(a_rows)], axis=1)
    return b_fold.astype(BF16), c_fold.astype(BF16), a_tile.astype(F32)


def _gelu_tanh(x):
    return 0.5 * x * (1.0 + jnp.tanh(math.sqrt(2.0 / math.pi) * (x + 0.044715 * (x * x * x))))


def _ssm_kernel(u_ref, bf_ref, cf_ref, a_ref, d_ref, wglu_ref, gn_ref, o_ref,
                lhs_ref, x_ref, y_ref, s_ref, *, chunk, batch):
    i = pl.program_id(0)
    d_ssm = u_ref.shape[-1]
    n_slab = d_ssm // LANES
    ns = x_ref.shape[-1] // 2
    rows = SUBLANES

    @pl.when(i == 0)
    def _():
        s_ref[...] = jnp.zeros_like(s_ref)

    zero = jnp.zeros((chunk, LANES), F32)
    for b in range(batch):
        ub = u_ref[b]
        for h in range(2):
            for c in range(n_slab):
                src = ub[:, c * LANES:(c + 1) * LANES] if (c * 2) // n_slab == h else zero
                lhs_ref[c, pl.ds(2 * b + h, chunk, stride=rows), :] = src
    n_blk = (rows * chunk) // SSM_MM_ROWS

    hs = ns // 2
    n_set = 2

    def in_mm(rb, carry):
        rs = pl.ds(pl.multiple_of(rb * SSM_MM_ROWS, SSM_MM_ROWS), SSM_MM_ROWS)
        for s in range(n_set):
            lhs = jnp.concatenate([lhs_ref[s, rs, :], lhs_ref[s + n_slab // 2, rs, :]],
                                  axis=1).astype(BF16)
            xs = jnp.dot(lhs, bf_ref[s], preferred_element_type=F32)
            x_ref[rs, s * hs:(s + 1) * hs] = xs[:, :hs]
            x_ref[rs, ns + s * hs:ns + (s + 1) * hs] = xs[:, hs:]
        return carry

    lax.fori_loop(0, n_blk, in_mm, 0)

    a_re = a_ref[:, :ns]
    a_im = a_ref[:, ns:]

    def step(t, carry):
        s_re, s_im = carry
        r0 = pl.multiple_of(t * rows, rows)
        n_re = a_re * s_re - a_im * s_im + x_ref[pl.ds(r0, rows), :ns]
        n_im = a_re * s_im + a_im * s_re + x_ref[pl.ds(r0, rows), ns:]
        x_ref[pl.ds(r0, rows), :ns] = n_re
        x_ref[pl.ds(r0, rows), ns:] = n_im
        return n_re, n_im

    s_re, s_im = lax.fori_loop(0, chunk, step, (s_ref[:, :ns], s_ref[:, ns:]), unroll=2)
    s_ref[:, :ns] = s_re
    s_ref[:, ns:] = s_im

    def out_mm(rb, carry):
        rs = pl.ds(pl.multiple_of(rb * SSM_MM_ROWS, SSM_MM_ROWS), SSM_MM_ROWS)
        for s in range(n_set):
            st = jnp.concatenate([x_ref[rs, s * hs:(s + 1) * hs],
                                  x_ref[rs, ns + s * hs:ns + (s + 1) * hs]], axis=1).astype(BF16)
            y = jnp.dot(st, cf_ref[s], preferred_element_type=F32)
            y_ref[s, rs, :] = y[:, :LANES]
            y_ref[s + n_slab // 2, rs, :] = y[:, LANES:]
        return carry

    lax.fori_loop(0, n_blk, out_mm, 0)

    for b in range(batch):
        parts = []
        for c in range(n_slab):
            h = (c * 2) // n_slab
            parts.append(y_ref[c, pl.ds(2 * b + h, chunk, stride=rows), :])
        yb = jnp.concatenate(parts, axis=1) + d_ref[...] * u_ref[b]
        g = jnp.dot(_gelu_tanh(yb).astype(BF16), wglu_ref[...], preferred_element_type=F32)
        z = g[:, :d_ssm] * jax.nn.sigmoid(g[:, d_ssm:])
        ms = jnp.mean(z * z, axis=1, keepdims=True)
        o_ref[b] = (z * lax.rsqrt(ms + RMS_EPS) * gn_ref[...]).astype(o_ref.dtype)


def _ssm(u, b_fold, c_fold, a_tile, d_skip, w_glu, norm_g):
    B, L, d_ssm = u.shape
    assert 2 * B == SUBLANES, "row-stream layout packs batch x 2 group halves on 8 sublanes"
    chunk = SSM_CHUNK
    ns2 = a_tile.shape[1]
    n_slab = d_ssm // LANES
    assert n_slab == 4, "two group sets x two group halves of 128 channels"
    ns, hs = ns2 // 2, ns2 // 4
    ch = lambda s: jnp.r_[s * LANES:(s + 1) * LANES, (s + 2) * LANES:(s + 3) * LANES]
    st = lambda s: jnp.r_[s * hs:(s + 1) * hs, ns + s * hs:ns + (s + 1) * hs]
    b_sets = jnp.stack([b_fold[ch(s)][:, st(s)] for s in range(2)])
    c_sets = jnp.stack([c_fold[st(s)][:, ch(s)] for s in range(2)])
    full = lambda shape: pl.BlockSpec(shape, lambda i: (0,) * len(shape))
    kern = functools.partial(_ssm_kernel, chunk=chunk, batch=B)
    return pl.pallas_call(
        kern,
        grid=(L // chunk,),
        in_specs=[pl.BlockSpec((B, chunk, d_ssm), lambda i: (0, i, 0)),
                  full(b_sets.shape), full(c_sets.shape), full((SUBLANES, ns2)),
                  full((1, d_ssm)), full((d_ssm, 2 * d_ssm)), full((1, d_ssm))],
        out_specs=pl.BlockSpec((B, chunk, d_ssm), lambda i: (0, i, 0)),
        out_shape=jax.ShapeDtypeStruct((B, L, d_ssm), BF16),
        scratch_shapes=[pltpu.VMEM((n_slab, SUBLANES * chunk, LANES), F32),
                        pltpu.VMEM((SUBLANES * chunk, ns2), F32),
                        pltpu.VMEM((n_slab, SUBLANES * chunk, LANES), F32),
                        pltpu.VMEM((SUBLANES, ns2), F32)],
        compiler_params=_params("arbitrary"),
        name="s5_ssm",
    )(u, b_sets, c_sets, a_tile, d_skip.reshape(1, d_ssm).astype(F32), w_glu.astype(BF16),
      norm_g.reshape(1, d_ssm).astype(F32))


def _layer_norm(x, g, b):
    mu = jnp.mean(x, axis=1, keepdims=True)
    xc = x - mu
    var = jnp.mean(xc * xc, axis=1, keepdims=True)
    return xc * lax.rsqrt(var + LN_EPS) * g + b


def _slab_load(ref, n_rows, d, lead=()):
    return jnp.concatenate(
        [ref[lead + (pl.ds(c, n_rows, stride=SUBLANES), slice(None))] for c in range(d // LANES)],
        axis=1)


def _slab_store(ref, val):
    n_rows, d = val.shape
    for c in range(d // LANES):
        ref[pl.ds(c, n_rows, stride=SUBLANES), :] = val[:, c * LANES:(c + 1) * LANES]


def _mix_kernel(x_ref, attn_ref, ssm_ref, woa_ref, wos_ref, g_ref, b_ref, wrt_ref, br_ref,
                x1_ref, idx_ref, gate_ref, cnt_ref, carry_ref):
    i = pl.program_id(0)
    tm = x_ref.shape[0]

    @pl.when(i == 0)
    def _():
        carry_ref[...] = jnp.zeros_like(carry_ref)

    mix = (jnp.dot(attn_ref[...], woa_ref[...], preferred_element_type=F32)
           + jnp.dot(ssm_ref[...], wos_ref[...], preferred_element_type=F32))
    x1 = _layer_norm(DEEPNORM_ALPHA * x_ref[...] + mix, g_ref[...], b_ref[...])
    _slab_store(x1_ref, x1)

    logits = lax.dot_general(wrt_ref[...], x1.astype(BF16), (((1,), (1,)), ((), ())),
                             preferred_element_type=F32) + br_ref[...]
    e_iota = lax.broadcasted_iota(jnp.int32, logits.shape, 0).astype(F32)
    work = logits
    vals, hots = [], []
    for k in range(TOP_K):
        mx = jnp.max(work, axis=0, keepdims=True)
        idx = jnp.min(jnp.where(work == mx, e_iota, float(N_EXPERTS)), axis=0, keepdims=True)
        hot = e_iota == idx
        work = jnp.where(hot, -jnp.inf, work)
        vals.append(mx)
        hots.append(hot)
        idx_ref[k:k + 1, :] = idx.astype(jnp.int32)
    ex = [jnp.exp(v - vals[0]) for v in vals]
    den = ex[0] + ex[1] + ex[2] + ex[3]
    for k in range(TOP_K):
        gate_ref[k:k + 1, :] = ex[k] / den

    sel = (hots[0] | hots[1] | hots[2] | hots[3]).astype(F32)
    carry_ref[...] = carry_ref[...] + jnp.sum(sel, axis=1, keepdims=True)
    cnt_ref[...] = carry_ref[...]


def _mix_route(x2, attn2, ssm2, w_out, ln_g, ln_b, w_router, b_router):
    T, D = x2.shape
    tm = MIX_ROWS
    d_attn = attn2.shape[1]
    wo = w_out.astype(BF16)
    full = lambda shape: pl.BlockSpec(shape, lambda i: (0,) * len(shape))
    rowblk = lambda w: pl.BlockSpec((tm, w), lambda i: (i, 0))
    colblk = pl.BlockSpec((TOP_K, tm), lambda i: (0, i))
    return pl.pallas_call(
        _mix_kernel,
        grid=(T // tm,),
        in_specs=[rowblk(D), rowblk(d_attn), rowblk(ssm2.shape[1]),
                  full((d_attn, D)), full((D - d_attn, D)), full((1, D)), full((1, D)),
                  full((N_EXPERTS, D)), full((N_EXPERTS, 1))],
        out_specs=[pl.BlockSpec((tm * SUBLANES, LANES), lambda i: (i, 0)),
                   colblk, colblk, full((N_EXPERTS, LANES))],
        out_shape=[jax.ShapeDtypeStruct((T * SUBLANES, LANES), F32),
                   jax.ShapeDtypeStruct((TOP_K, T), jnp.int32),
                   jax.ShapeDtypeStruct((TOP_K, T), F32),
                   jax.ShapeDtypeStruct((N_EXPERTS, LANES), F32)],
        scratch_shapes=[pltpu.VMEM((N_EXPERTS, LANES), F32)],
        compiler_params=_params("arbitrary"),
        name="mix_route",
    )(x2, attn2, ssm2, wo[:d_attn], wo[d_attn:], ln_g.reshape(1, D).astype(F32),
      ln_b.reshape(1, D).astype(F32), w_router.T.astype(BF16),
      b_router.reshape(N_EXPERTS, 1).astype(F32))


def _expert_kernel(be_ref, nu_ref, src_ref, cnt_ref, gsrc_ref, sdst_ref, x1_ref, wgu_ref, bg_ref,
                   bl_ref, wd_ref, bd_ref, o4_ref, sel_ref, wgl_ref, wdb_ref, xbuf, obuf, ibuf,
                   gsem, ssem, isem):
    i = pl.program_id(0)
    nu = nu_ref[0]
    _, d, f2 = wgu_ref.shape
    f = f2 // 2
    half = SPLIT_COLS // 2
    active = i < nu
    blk_rows = MOE_ROWS * SUBLANES

    def idx_copies(j):
        base = pl.multiple_of(lax.shift_right_logical(src_ref[j], IDX_SHIFT) * IDX_ALIGN, IDX_ALIGN)
        b = lax.rem(j, 3)
        dst = lambda h: ibuf.at[pl.ds(pl.multiple_of((2 * b + h) * IDX_WIN, IDX_WIN), IDX_WIN)]
        return (pltpu.make_async_copy(gsrc_ref.at[pl.ds(base, IDX_WIN)], dst(0), isem.at[b]),
                pltpu.make_async_copy(sdst_ref.at[pl.ds(base, IDX_WIN)], dst(1), isem.at[b]))

    def entry_base(j, h):
        return (2 * lax.rem(j, 3) + h) * IDX_WIN + (src_ref[j] & (IDX_ALIGN - 1))

    def rows(start, n=SUBLANES):
        if isinstance(start, int):
            return pl.ds(start, n)
        return pl.ds(pl.multiple_of(start, SUBLANES), n)

    def gather_row(slot, ebase, r):
        return pltpu.make_async_copy(x1_ref.at[rows(ibuf[ebase + r])],
                                     xbuf.at[rows(slot * blk_rows + r * SUBLANES)],
                                     gsem.at[slot])

    def scatter_row(slot, ebase, r):
        return pltpu.make_async_copy(obuf.at[rows(slot * blk_rows + r * SUBLANES)],
                                     o4_ref.at[rows(ibuf[ebase + r])], ssem.at[slot])

    def start_block(j, row_copy):
        for slot in range(2):
            @pl.when((j & 1) == slot)
            def _():
                for r in range(MOE_ROWS):
                    row_copy(slot, r).start(priority=r % 2)

    def start_gather(j):
        ebase = entry_base(j, 0)
        start_block(j, lambda slot, r: gather_row(slot, ebase, r))

    def wait_gather(j):
        pltpu.make_async_copy(x1_ref.at[rows(0, blk_rows)],
                              xbuf.at[rows((j & 1) * blk_rows, blk_rows)], gsem.at[j & 1]).wait()

    def start_scatter(j):
        ebase = entry_base(j, 1)

        @pl.when(cnt_ref[j] == MOE_ROWS)
        def _():
            start_block(j, lambda slot, r: scatter_row(slot, ebase, r))

        @pl.when(cnt_ref[j] < MOE_ROWS)
        def _():
            def one(r, c):
                scatter_row(j & 1, ebase, r).start()
                return c

            lax.fori_loop(0, cnt_ref[j], one, 0)

    def wait_scatter(j):
        @pl.when(cnt_ref[j] == MOE_ROWS)
        def _():
            pltpu.make_async_copy(obuf.at[rows((j & 1) * blk_rows, blk_rows)],
                                  o4_ref.at[rows(0, blk_rows)], ssem.at[j & 1]).wait()

        @pl.when(cnt_ref[j] < MOE_ROWS)
        def _():
            def one(r, c):
                pltpu.make_async_copy(obuf.at[rows(0)], o4_ref.at[rows(0)], ssem.at[j & 1]).wait()
                return c

            lax.fori_loop(0, cnt_ref[j], one, 0)

    @pl.when(i == 0)
    def _():
        r = lax.broadcasted_iota(jnp.int32, sel_ref.shape, 0)
        c = lax.broadcasted_iota(jnp.int32, sel_ref.shape, 1)
        src = jnp.where(c < half, 2 * c, 2 * (c - half) + 1)
        sel_ref[...] = (r == src).astype(BF16)
        for cp in idx_copies(0):
            cp.start()
        for cp in idx_copies(0):
            cp.wait()

        @pl.when(nu > 1)
        def _():
            for cp in idx_copies(1):
                cp.start()

        ebase0 = entry_base(0, 0)

        def first(r, c):
            gather_row(0, ebase0, r).start()
            return c

        lax.fori_loop(0, MOE_ROWS, first, 0)

    @pl.when(active)
    def _():
        @pl.when(i + 2 < nu)
        def _():
            for cp in idx_copies(i + 2):
                cp.start()

        wait_gather(i)

        @pl.when(i + 1 < nu)
        def _():
            for cp in idx_copies(i + 1):
                cp.wait()
            start_gather(i + 1)

    new_expert = (i == 0) | (be_ref[i] != be_ref[jnp.maximum(i - 1, 0)])

    @pl.when(active & new_expert)
    def _():
        def split_rows(rb, carry):
            rs = pl.ds(pl.multiple_of(rb * WPREP_ROWS, WPREP_ROWS), WPREP_ROWS)
            for j in range(f2 // SPLIT_COLS):
                w = wgu_ref[0, rs, j * SPLIT_COLS:(j + 1) * SPLIT_COLS].astype(BF16)
                y = jnp.dot(w, sel_ref[...], preferred_element_type=F32).astype(BF16)
                wgl_ref[rs, j * half:(j + 1) * half] = y[:, :half]
                wgl_ref[rs, f + j * half:f + (j + 1) * half] = y[:, half:]
            return carry

        def cast_rows(rb, carry):
            rs = pl.ds(pl.multiple_of(rb * WPREP_ROWS, WPREP_ROWS), WPREP_ROWS)
            wdb_ref[rs, :] = wd_ref[0, rs, :].astype(BF16)
            return carry

        lax.fori_loop(0, d // WPREP_ROWS, split_rows, 0)
        lax.fori_loop(0, f // WPREP_ROWS, cast_rows, 0)

    @pl.when(active)
    def _():
        base = pl.multiple_of((i & 1) * blk_rows, blk_rows)
        xb = _slab_load(xbuf.at[pl.ds(base, blk_rows)], MOE_ROWS, d).astype(BF16)
        h = jnp.dot(xb, wgl_ref[...], preferred_element_type=F32)
        xg = jnp.minimum(h[:, :f] + bg_ref[0], SWIGLU_LIMIT)
        xl = jnp.clip(h[:, f:] + bl_ref[0], -SWIGLU_LIMIT, SWIGLU_LIMIT)
        act = xg * jax.nn.sigmoid(SWIGLU_ALPHA * xg) * (xl + 1.0)
        out = jnp.dot(act.astype(BF16), wdb_ref[...], preferred_element_type=F32) + bd_ref[0]
        _slab_store(obuf.at[pl.ds(base, blk_rows)], out)

        start_scatter(i)

        @pl.when(i >= 1)
        def _():
            wait_scatter(i - 1)

        @pl.when(i == nu - 1)
        def _():
            wait_scatter(i)


def _experts(x1s, gsrc, sdst, block_e, n_used, blk_src, blk_cnt, w_gate_up, b_gate_up, w_down,
             b_down):
    E, D, F2 = w_gate_up.shape
    F = F2 // 2
    tm = MOE_ROWS
    n_blocks = block_e.shape[0]
    bg = b_gate_up[:, 0::2].reshape(E, 1, F).astype(F32)
    bl = b_gate_up[:, 1::2].reshape(E, 1, F).astype(F32)
    bd = b_down.reshape(E, 1, D).astype(F32)
    wmap = lambda i, be, nu, sr, cn: (be[i], 0, 0)
    any_spec = pl.BlockSpec(memory_space=pl.ANY)
    return pl.pallas_call(
        _expert_kernel,
        grid_spec=pltpu.PrefetchScalarGridSpec(
            num_scalar_prefetch=4,
            grid=(n_blocks,),
            in_specs=[any_spec, any_spec, any_spec,
                      pl.BlockSpec((1, D, F2), wmap),
                      pl.BlockSpec((1, 1, F), wmap), pl.BlockSpec((1, 1, F), wmap),
                      pl.BlockSpec((1, F, D), wmap), pl.BlockSpec((1, 1, D), wmap)],
            out_specs=any_spec,
            scratch_shapes=[pltpu.VMEM((SPLIT_COLS, SPLIT_COLS), BF16),
                            pltpu.VMEM((D, F2), BF16),
                            pltpu.VMEM((F, D), BF16),
                            pltpu.VMEM((2 * tm * SUBLANES, LANES), F32),
                            pltpu.VMEM((2 * tm * SUBLANES, LANES), F32),
                            pltpu.SMEM((3 * 2 * IDX_WIN,), jnp.int32),
                            pltpu.SemaphoreType.DMA((2,)), pltpu.SemaphoreType.DMA((2,)),
                            pltpu.SemaphoreType.DMA((3,))]),
        out_shape=jax.ShapeDtypeStruct((TOP_K * x1s.shape[0], LANES), F32),
        compiler_params=_params("arbitrary"),
        name="moe_experts",
    )(block_e, n_used, blk_src, blk_cnt, gsrc, sdst, x1s, w_gate_up, bg, bl, w_down, bd)


def _combine_kernel(gate_ref, x1_ref, o4_ref, p_ref, wpg_ref, wpp_ref, g_ref, b_ref, o_ref):
    tm, d = o_ref.shape
    pp = jnp.dot(p_ref[...].astype(BF16), wpp_ref[...], preferred_element_type=F32)
    gates = gate_ref[...]
    moe = gates[:, 0:1] * _slab_load(o4_ref, tm, d, lead=(0,))
    for k in range(1, TOP_K):
        moe = moe + gates[:, k:k + 1] * _slab_load(o4_ref, tm, d, lead=(k,))
    r = DEEPNORM_ALPHA * _slab_load(x1_ref, tm, d) + moe
    gate = jax.nn.sigmoid(jnp.dot(r.astype(BF16), wpg_ref[...], preferred_element_type=F32))
    o_ref[...] = _layer_norm(r + gate * pp, g_ref[...], b_ref[...])


def _combine(gates_t, x1s, out4, p2, w_ple_gate, w_ple_proj, ln_g, ln_b):
    T, pd = p2.shape
    D = w_ple_gate.shape[0]
    tm = COMBINE_ROWS
    full = lambda shape: pl.BlockSpec(shape, lambda i: (0,) * len(shape))
    return pl.pallas_call(
        _combine_kernel,
        grid=(T // tm,),
        in_specs=[pl.BlockSpec((tm, TOP_K), lambda i: (i, 0)),
                  pl.BlockSpec((tm * SUBLANES, LANES), lambda i: (i, 0)),
                  pl.BlockSpec((TOP_K, tm * SUBLANES, LANES), lambda i: (0, i, 0)),
                  pl.BlockSpec((tm, pd), lambda i: (i, 0)),
                  full((D, D)), full((pd, D)), full((1, D)), full((1, D))],
        out_specs=pl.BlockSpec((tm, D), lambda i: (i, 0)),
        out_shape=jax.ShapeDtypeStruct((T, D), F32),
        compiler_params=_params("parallel"),
        name="moe_combine",
    )(gates_t, x1s, out4, p2, w_ple_gate.astype(BF16), w_ple_proj.astype(BF16),
      ln_g.reshape(1, D).astype(F32), ln_b.reshape(1, D).astype(F32))


def kernel(x, p, w_in, lambda_q1, lambda_k1, lambda_q2, lambda_k2, subln_g, ssm_a_re, ssm_a_im,
           ssm_log_dt, ssm_b_re, ssm_b_im, ssm_c_re, ssm_c_im, ssm_d, w_glu, ssm_norm_g, w_out,
           ln1_g, ln1_b, w_router, b_router, w_gate_up, b_gate_up, w_down, b_down, w_ple_gate,
           w_ple_proj, ln2_g, ln2_b):
    B, L, D = x.shape
    T = B * L
    assert D == SUBLANES * LANES, "token-slab layout holds one token per (8, 128) tile"
    for i in range(DEPTH):
        lambda_init = 0.8 - 0.6 * math.exp(-0.3 * i)
        q, kt, v, u = _in_proj(x, w_in[i])
        lam_vecs = jnp.stack([lambda_q1[i], lambda_k1[i], lambda_q2[i], lambda_k2[i]]).astype(F32)
        attn = _attention(q, kt, v, lam_vecs, subln_g[i], lambda_init)
        b_fold, c_fold, a_tile = _ssm_fold_params(ssm_a_re[i], ssm_a_im[i], ssm_log_dt[i],
                                                  ssm_b_re[i], ssm_b_im[i], ssm_c_re[i],
                                                  ssm_c_im[i], B)
        ssm = _ssm(u, b_fold, c_fold, a_tile, ssm_d[i], w_glu[i], ssm_norm_g[i])

        x1, idx, gates, counts = _mix_route(
            x.reshape(T, D), attn.reshape(T, -1), ssm.reshape(T, -1), w_out[i], ln1_g[i],
            ln1_b[i], w_router[i], b_router[i])

        n_assign = T * TOP_K
        keyed = idx.T.reshape(-1) * n_assign + jnp.arange(n_assign, dtype=jnp.int32)
        order = lax.rem(lax.sort(keyed), n_assign)
        order = jnp.concatenate([order, jnp.zeros((IDX_WIN,), jnp.int32)])
        tok_row = lax.shift_right_logical(order, TOP_K_SHIFT) * SUBLANES
        gsrc = tok_row
        sdst = (order & (TOP_K - 1)) * (T * SUBLANES) + tok_row
        cnt = counts[:, 0].astype(jnp.int32)
        start = jnp.cumsum(cnt) - cnt
        n_blk_e = (cnt + MOE_ROWS - 1) // MOE_ROWS
        blk_end = jnp.cumsum(n_blk_e)
        n_blocks = (T * TOP_K) // MOE_ROWS + N_EXPERTS
        blk = jnp.arange(n_blocks, dtype=jnp.int32)
        block_e = jnp.minimum(jnp.sum(blk_end[None, :] <= blk[:, None], axis=1),
                              N_EXPERTS - 1).astype(jnp.int32)
        local = blk - (blk_end - n_blk_e)[block_e]
        blk_src = (start[block_e] + local * MOE_ROWS).astype(jnp.int32)
        blk_cnt = jnp.clip(cnt[block_e] - local * MOE_ROWS, 0, MOE_ROWS).astype(jnp.int32)
        n_used = blk_end[-1:].astype(jnp.int32)
        blk_src = jnp.where(blk < n_used, blk_src, 0)
        blk_cnt = jnp.where(blk < n_used, blk_cnt, 0)

        out4 = _experts(x1, gsrc, sdst, block_e, n_used, blk_src, blk_cnt, w_gate_up[i],
                        b_gate_up[i], w_down[i], b_down[i])
        x = _combine(gates.T, x1, out4.reshape(TOP_K, T * SUBLANES, LANES), p[i].reshape(T, -1), w_ple_gate[i], w_ple_proj[i],
                     ln2_g[i], ln2_b[i]).reshape(B, L, D)
    return x
```

```python
import functools
import math

import jax
import jax.numpy as jnp
from jax import lax
from jax.experimental import pallas as pl
from jax.experimental.pallas import tpu as pltpu

F32 = jnp.float32
BF16 = jnp.bfloat16

N_HEADS = 8
HEAD_DIM = 32
V_DIM = 2 * HEAD_DIM
D_ATTN = N_HEADS * V_DIM
SSM_GROUP = 16
SSM_STATE = 64
N_EXPERTS = 32
TOP_K = 4
TOP_K_SHIFT = 2
SWIGLU_LIMIT = 7.0
SWIGLU_ALPHA = 1.702
LN_EPS = 1e-5
RMS_EPS = 1e-5
DEPTH = 1
DEEPNORM_ALPHA = (2.0 * DEPTH) ** 0.25
LOG2E = math.log2(math.e)

LANES = 128
SUBLANES = 8
VMEM_LIMIT = 56 * 1024 * 1024

PROJ_ROWS = 512
ATTN_Q = 512
BIAS_PARTS = 3
SSM_CHUNK = 128
SSM_MM_ROWS = 256
MIX_ROWS = 512
MOE_ROWS = 256
SPLIT_COLS = 512
WPREP_ROWS = 256
IDX_WIN = 2048
IDX_ALIGN = 1024
IDX_SHIFT = 10
COMBINE_ROWS = 256


def _params(*sem):
    return pltpu.CompilerParams(dimension_semantics=sem, vmem_limit_bytes=VMEM_LIMIT)


def _in_proj_kernel(x_ref, wq_ref, wkt_ref, wv_ref, wu_ref, q_ref, kt_ref, v_ref, u_ref):
    xb = x_ref[0].astype(BF16)
    q = jnp.dot(xb, wq_ref[...], preferred_element_type=F32)
    q_ref[0] = (q * (HEAD_DIM ** -0.5 * LOG2E)).astype(BF16)
    kt = lax.dot_general(wkt_ref[...], xb, (((1,), (1,)), ((), ())),
                         preferred_element_type=F32)
    kt_ref[0, 0] = kt.astype(BF16)
    lane_v = lax.broadcasted_iota(jnp.int32, (1, wv_ref.shape[1]), 1)
    ones_lane = (lane_v % LANES == V_DIM).astype(F32)
    v_ref[0] = (jnp.dot(xb, wv_ref[...], preferred_element_type=F32) + ones_lane).astype(BF16)
    u_ref[0] = jnp.dot(xb, wu_ref[...], preferred_element_type=F32)


def _in_proj(x, w_in):
    B, L, D = x.shape
    d_ssm = w_in.shape[1] - 3 * D_ATTN
    tm = PROJ_ROWS
    nt = L // tm
    wb = w_in.astype(BF16)
    wq = wb[:, :D_ATTN]
    wkt = wb[:, D_ATTN:2 * D_ATTN].T
    wv = wb[:, 2 * D_ATTN:3 * D_ATTN].reshape(D, N_HEADS, V_DIM)
    wv = jnp.pad(wv, ((0, 0), (0, 0), (0, LANES - V_DIM))).reshape(D, N_HEADS * LANES)
    wu = wb[:, 3 * D_ATTN:]
    full = lambda shape: pl.BlockSpec(shape, lambda b, i: (0,) * len(shape))
    return pl.pallas_call(
        _in_proj_kernel,
        grid=(B, nt),
        in_specs=[pl.BlockSpec((1, tm, D), lambda b, i: (b, i, 0)),
                  full((D, D_ATTN)), full((D_ATTN, D)), full((D, N_HEADS * LANES)),
                  full((D, d_ssm))],
        out_specs=[pl.BlockSpec((1, tm, D_ATTN), lambda b, i: (b, i, 0)),
                   pl.BlockSpec((1, 1, D_ATTN, tm), lambda b, i: (b, i, 0, 0)),
                   pl.BlockSpec((1, tm, N_HEADS * LANES), lambda b, i: (b, i, 0)),
                   pl.BlockSpec((1, tm, d_ssm), lambda b, i: (b, i, 0))],
        out_shape=[jax.ShapeDtypeStruct((B, L, D_ATTN), BF16),
                   jax.ShapeDtypeStruct((B, nt, D_ATTN, tm), BF16),
                   jax.ShapeDtypeStruct((B, L, N_HEADS * LANES), BF16),
                   jax.ShapeDtypeStruct((B, L, d_ssm), F32)],
        compiler_params=_params("parallel", "parallel"),
        name="in_proj",
    )(x, wq, wkt, wv, wu)


def _attn_kernel(slope_ref, lam_ref, q_ref, kt_ref, v_ref, g_ref, o_ref,
                 q4_ref, jr_ref, p_ref, m_ref, acc_ref, *, tq, tk, lambda_init):
    hp = pl.program_id(1)
    qi = pl.program_id(2)
    q0 = qi * tq
    n_cb = tk // LANES

    q = q_ref[0]
    lane_q = lax.broadcasted_iota(jnp.int32, q.shape, 1)
    for c in range(4):
        rs = slice(c * tq, (c + 1) * tq)
        lo = (c // 2) * BIAS_PARTS
        q4_ref[rs, :LANES] = jnp.where(lane_q // HEAD_DIM == c, q, jnp.zeros_like(q))
        q4_ref[rs, LANES:] = ((lane_q >= lo) & (lane_q < lo + BIAS_PARTS)).astype(BF16)

    @pl.when(qi == 0)
    def _():
        colf = lax.broadcasted_iota(jnp.int32, (1, tk), 1).astype(F32)
        r_i = lax.broadcasted_iota(jnp.int32, (LANES, tk), 0)
        jr = jnp.zeros((LANES, tk), F32)
        for h in range(2):
            rem = (slope_ref[2 * hp + h] * LOG2E) * colf
            for part in range(BIAS_PARTS):
                piece = rem.astype(BF16).astype(F32)
                jr = jnp.where(r_i == h * BIAS_PARTS + part, piece, jr)
                rem = rem - piece
        jr_ref[...] = jr.astype(BF16)

    m_ref[...] = jnp.full(m_ref.shape, -jnp.inf, F32)
    acc_ref[...] = jnp.zeros(acc_ref.shape, F32)

    col = lax.broadcasted_iota(jnp.int32, (1, tk), 1)
    row = lax.broadcasted_iota(jnp.int32, (tq, 1), 0)
    zero_row = jnp.zeros((1, LANES), jnp.int32)

    def v_tile(t):
        return v_ref[0, pl.ds(pl.multiple_of(t * tk, tk), tk), :]

    def tile(t, masked):
        kt_aug = jnp.concatenate([kt_ref[0, t], jr_ref[...]], axis=0)
        s_all = jnp.dot(q4_ref[...], kt_aug, preferred_element_type=F32)
        rel = t * tk - q0 + col
        base = (t * tk - q0 + zero_row).astype(F32)
        for c in range(4):
            rs = slice(c * tq, (c + 1) * tq)
            off = (slope_ref[2 * hp + c // 2] * LOG2E) * base
            s = s_all[rs]
            if masked:
                s = jnp.where(rel <= row, s, -jnp.inf)
            blocks = [s[:, cb * LANES:(cb + 1) * LANES] for cb in range(n_cb)]
            part = blocks[0]
            for blk in blocks[1:]:
                part = jnp.maximum(part, blk)
            m_old = m_ref[rs]
            m_new = jnp.maximum(m_old, jnp.max(part, axis=1, keepdims=True) + off)
            m_ref[rs] = m_new
            alpha = jnp.exp2(m_old - m_new)
            shift = m_new - off
            acc_ref[rs] = alpha * acc_ref[rs]
            p_ref[rs] = jnp.concatenate([jnp.exp2((blk - shift).astype(BF16)) for blk in blocks],
                                        axis=1)
        v2 = v_tile(t)
        for j in range(2):
            hr = slice(2 * j * tq, (2 * j + 2) * tq)
            acc_ref[hr] += jnp.dot(p_ref[hr], v2[:, j * LANES:(j + 1) * LANES],
                                   preferred_element_type=F32)

    n_full = q0 // tk

    def body(t, carry):
        tile(t, False)
        return carry

    lax.fori_loop(0, n_full, body, 0)
    tile(n_full, True)
    acc = acc_ref[...]

    lam = lam_ref[...]
    lane = lax.broadcasted_iota(jnp.int32, (1, LANES), 1)
    outs = []
    for j in range(2):
        o = []
        for c in (2 * j, 2 * j + 1):
            a = acc[c * tq:(c + 1) * tq]
            o.append(a / a[:, V_DIM:V_DIM + 1])
        d = o[0] - lam * o[1]
        ms = jnp.sum(jnp.where(lane < V_DIM, d * d, 0.0), axis=1, keepdims=True) * (1.0 / V_DIM)
        outs.append(d * lax.rsqrt(ms + RMS_EPS))
    out = jnp.where(lane < V_DIM, outs[0], pltpu.roll(outs[1], V_DIM, 1))
    out = out * g_ref[...] * (1.0 - lambda_init)
    o_ref[0] = out.astype(o_ref.dtype)


def _attention(q, kt, v, lam_vecs, subln_g, lambda_init):
    B, L, _ = q.shape
    tq, tk = ATTN_Q, PROJ_ROWS
    nk = L // tk
    slopes = jnp.exp2(-(jnp.arange(N_HEADS, dtype=F32) + 1.0) * (8.0 / N_HEADS))
    lam = (jnp.exp(jnp.sum(lam_vecs[0] * lam_vecs[1])) - jnp.exp(jnp.sum(lam_vecs[2] * lam_vecs[3]))
           + lambda_init).reshape(1, 1).astype(F32)
    g2 = jnp.tile(subln_g.astype(F32), 2).reshape(1, LANES)
    kern = functools.partial(_attn_kernel, tq=tq, tk=tk, lambda_init=lambda_init)
    return pl.pallas_call(
        kern,
        grid_spec=pltpu.PrefetchScalarGridSpec(
            num_scalar_prefetch=1,
            grid=(B, N_HEADS // 2, L // tq),
            in_specs=[pl.BlockSpec((1, 1), lambda b, h, i, s: (0, 0)),
                      pl.BlockSpec((1, tq, LANES), lambda b, h, i, s: (b, i, h)),
                      pl.BlockSpec((1, nk, LANES, tk), lambda b, h, i, s: (b, 0, h, 0)),
                      pl.BlockSpec((1, L, 2 * LANES), lambda b, h, i, s: (b, 0, h)),
                      pl.BlockSpec((1, LANES), lambda b, h, i, s: (0, 0))],
            out_specs=pl.BlockSpec((1, tq, LANES), lambda b, h, i, s: (b, i, h)),
            scratch_shapes=[pltpu.VMEM((4 * tq, 2 * LANES), BF16),
                            pltpu.VMEM((LANES, tk), BF16),
                            pltpu.VMEM((4 * tq, tk), BF16),
                            pltpu.VMEM((4 * tq, LANES), F32),
                            pltpu.VMEM((4 * tq, LANES), F32)]),
        out_shape=jax.ShapeDtypeStruct((B, L, D_ATTN), BF16),
        compiler_params=_params("parallel", "parallel", "arbitrary"),
        name="diff_attn",
    )(slopes, lam, q, kt, v, g2)


def _ssm_fold_params(a_re, a_im, log_dt, b_re, b_im, c_re, c_im, batch):
    G, P = a_re.shape
    C = b_re.shape[-1]
    half = G // 2
    A = lax.complex(a_re.astype(F32), a_im.astype(F32))
    dt = jnp.exp(log_dt.astype(F32))[:, None]
    a_bar = jnp.exp(A * dt)
    b_bar = ((a_bar - 1.0) / A)[..., None] * lax.complex(b_re.astype(F32), b_im.astype(F32))
    sel = jnp.eye(half, dtype=F32)[jnp.arange(G) % half]
    fold_b = lambda m: jnp.einsum('gpc,gk->gckp', m, sel).reshape(G * C, half * P)
    b_fold = jnp.concatenate([fold_b(jnp.real(b_bar)), fold_b(jnp.imag(b_bar))], axis=1)
    fold_c = lambda m: jnp.einsum('gcp,gk->kpgc', m, sel).reshape(half * P, G * C)
    c_fold = jnp.concatenate([fold_c(c_re.astype(F32)), -fold_c(c_im.astype(F32))], axis=0)
    a_rows = jnp.tile(a_bar.reshape(2, half * P), (batch, 1))
    a_tile = jnp.concatenate([jnp.real(a_rows), jnp.imag(a_rows)], axis=1)
    return b_fold.astype(BF16), c_fold.astype(BF16), a_tile.astype(F32)


def _gelu_tanh(x):
    return 0.5 * x * (1.0 + jnp.tanh(math.sqrt(2.0 / math.pi) * (x + 0.044715 * (x * x * x))))


def _ssm_kernel(u_ref, bf_ref, cf_ref, a_ref, d_ref, wglu_ref, gn_ref, o_ref,
                lhs_ref, x_ref, y_ref, s_ref, *, chunk, batch):
    i = pl.program_id(0)
    d_ssm = u_ref.shape[-1]
    n_slab = d_ssm // LANES
    ns = x_ref.shape[-1] // 2
    rows = SUBLANES

    @pl.when(i == 0)
    def _():
        s_ref[...] = jnp.zeros_like(s_ref)

    zero = jnp.zeros((chunk, LANES), F32)
    for b in range(batch):
        ub = u_ref[b]
        for h in range(2):
            for c in range(n_slab):
                src = ub[:, c * LANES:(c + 1) * LANES] if (c * 2) // n_slab == h else zero
                lhs_ref[c, pl.ds(2 * b + h, chunk, stride=rows), :] = src
    n_blk = (rows * chunk) // SSM_MM_ROWS

    hs = ns // 2
    n_set = 2

    def in_mm(rb, carry):
        rs = pl.ds(pl.multiple_of(rb * SSM_MM_ROWS, SSM_MM_ROWS), SSM_MM_ROWS)
        for s in range(n_set):
            lhs = jnp.concatenate([lhs_ref[s, rs, :], lhs_ref[s + n_slab // 2, rs, :]],
                                  axis=1).astype(BF16)
            xs = jnp.dot(lhs, bf_ref[s], preferred_element_type=F32)
            x_ref[rs, s * hs:(s + 1) * hs] = xs[:, :hs]
            x_ref[rs, ns + s * hs:ns + (s + 1) * hs] = xs[:, hs:]
        return carry

    lax.fori_loop(0, n_blk, in_mm, 0)

    a_re = a_ref[:, :ns]
    a_im = a_ref[:, ns:]

    def step(t, carry):
        s_re, s_im = carry
        r0 = pl.multiple_of(t * rows, rows)
        n_re = a_re * s_re - a_im * s_im + x_ref[pl.ds(r0, rows), :ns]
        n_im = a_re * s_im + a_im * s_re + x_ref[pl.ds(r0, rows), ns:]
        x_ref[pl.ds(r0, rows), :ns] = n_re
        x_ref[pl.ds(r0, rows), ns:] = n_im
        return n_re, n_im

    s_re, s_im = lax.fori_loop(0, chunk, step, (s_ref[:, :ns], s_ref[:, ns:]), unroll=2)
    s_ref[:, :ns] = s_re
    s_ref[:, ns:] = s_im

    def out_mm(rb, carry):
        rs = pl.ds(pl.multiple_of(rb * SSM_MM_ROWS, SSM_MM_ROWS), SSM_MM_ROWS)
        for s in range(n_set):
            st = jnp.concatenate([x_ref[rs, s * hs:(s + 1) * hs],
                                  x_ref[rs, ns + s * hs:ns + (s + 1) * hs]], axis=1).astype(BF16)
            y = jnp.dot(st, cf_ref[s], preferred_element_type=F32)
            y_ref[s, rs, :] = y[:, :LANES]
            y_ref[s + n_slab // 2, rs, :] = y[:, LANES:]
        return carry

    lax.fori_loop(0, n_blk, out_mm, 0)

    for b in range(batch):
        parts = []
        for c in range(n_slab):
            h = (c * 2) // n_slab
            parts.append(y_ref[c, pl.ds(2 * b + h, chunk, stride=rows), :])
        yb = jnp.concatenate(parts, axis=1) + d_ref[...] * u_ref[b]
        g = jnp.dot(_gelu_tanh(yb).astype(BF16), wglu_ref[...], preferred_element_type=F32)
        z = g[:, :d_ssm] * jax.nn.sigmoid(g[:, d_ssm:])
        ms = jnp.mean(z * z, axis=1, keepdims=True)
        o_ref[b] = (z * lax.rsqrt(ms + RMS_EPS) * gn_ref[...]).astype(o_ref.dtype)


def _ssm(u, b_fold, c_fold, a_tile, d_skip, w_glu, norm_g):
    B, L, d_ssm = u.shape
    assert 2 * B == SUBLANES, "row-stream layout packs batch x 2 group halves on 8 sublanes"
    chunk = SSM_CHUNK
    ns2 = a_tile.shape[1]
    n_slab = d_ssm // LANES
    assert n_slab == 4, "two group sets x two group halves of 128 channels"
    ns, hs = ns2 // 2, ns2 // 4
    ch = lambda s: jnp.r_[s * LANES:(s + 1) * LANES, (s + 2) * LANES:(s + 3) * LANES]
    st = lambda s: jnp.r_[s * hs:(s + 1) * hs, ns + s * hs:ns + (s + 1) * hs]
    b_sets = jnp.stack([b_fold[ch(s)][:, st(s)] for s in range(2)])
    c_sets = jnp.stack([c_fold[st(s)][:, ch(s)] for s in range(2)])
    full = lambda shape: pl.BlockSpec(shape, lambda i: (0,) * len(shape))
    kern = functools.partial(_ssm_kernel, chunk=chunk, batch=B)
    return pl.pallas_call(
        kern,
        grid=(L // chunk,),
        in_specs=[pl.BlockSpec((B, chunk, d_ssm), lambda i: (0, i, 0)),
                  full(b_sets.shape), full(c_sets.shape), full((SUBLANES, ns2)),
                  full((1, d_ssm)), full((d_ssm, 2 * d_ssm)), full((1, d_ssm))],
        out_specs=pl.BlockSpec((B, chunk, d_ssm), lambda i: (0, i, 0)),
        out_shape=jax.ShapeDtypeStruct((B, L, d_ssm), BF16),
        scratch_shapes=[pltpu.VMEM((n_slab, SUBLANES * chunk, LANES), F32),
                        pltpu.VMEM((SUBLANES * chunk, ns2), F32),
                        pltpu.VMEM((n_slab, SUBLANES * chunk, LANES), F32),
                        pltpu.VMEM((SUBLANES, ns2), F32)],
        compiler_params=_params("arbitrary"),
        name="s5_ssm",
    )(u, b_sets, c_sets, a_tile, d_skip.reshape(1, d_ssm).astype(F32), w_glu.astype(BF16),
      norm_g.reshape(1, d_ssm).astype(F32))


def _layer_norm(x, g, b):
    mu = jnp.mean(x, axis=1, keepdims=True)
    xc = x - mu
    var = jnp.mean(xc * xc, axis=1, keepdims=True)
    return xc * lax.rsqrt(var + LN_EPS) * g + b


def _slab_load(ref, n_rows, d, lead=()):
    return jnp.concatenate(
        [ref[lead + (pl.ds(c, n_rows, stride=SUBLANES), slice(None))] for c in range(d // LANES)],
        axis=1)


def _slab_store(ref, val):
    n_rows, d = val.shape
    for c in range(d // LANES):
        ref[pl.ds(c, n_rows, stride=SUBLANES), :] = val[:, c * LANES:(c + 1) * LANES]


def _mix_kernel(x_ref, attn_ref, ssm_ref, woa_ref, wos_ref, g_ref, b_ref, wrt_ref, br_ref,
                x1_ref, idx_ref, gate_ref, cnt_ref, carry_ref):
    i = pl.program_id(0)
    tm = x_ref.shape[0]

    @pl.when(i == 0)
    def _():
        carry_ref[...] = jnp.zeros_like(carry_ref)

    mix = (jnp.dot(attn_ref[...], woa_ref[...], preferred_element_type=F32)
           + jnp.dot(ssm_ref[...], wos_ref[...], preferred_element_type=F32))
    x1 = _layer_norm(DEEPNORM_ALPHA * x_ref[...] + mix, g_ref[...], b_ref[...])
    _slab_store(x1_ref, x1)

    logits = lax.dot_general(wrt_ref[...], x1.astype(BF16), (((1,), (1,)), ((), ())),
                             preferred_element_type=F32) + br_ref[...]
    e_iota = lax.broadcasted_iota(jnp.int32, logits.shape, 0).astype(F32)
    work = logits
    vals, hots = [], []
    for k in range(TOP_K):
        mx = jnp.max(work, axis=0, keepdims=True)
        idx = jnp.min(jnp.where(work == mx, e_iota, float(N_EXPERTS)), axis=0, keepdims=True)
        hot = e_iota == idx
        work = jnp.where(hot, -jnp.inf, work)
        vals.append(mx)
        hots.append(hot)
        idx_ref[k:k + 1, :] = idx.astype(jnp.int32)
    ex = [jnp.exp(v - vals[0]) for v in vals]
    den = ex[0] + ex[1] + ex[2] + ex[3]
    for k in range(TOP_K):
        gate_ref[k:k + 1, :] = ex[k] / den

    sel = (hots[0] | hots[1] | hots[2] | hots[3]).astype(F32)
    carry_ref[...] = carry_ref[...] + jnp.sum(sel, axis=1, keepdims=True)
    cnt_ref[...] = carry_ref[...]


def _mix_route(x2, attn2, ssm2, w_out, ln_g, ln_b, w_router, b_router):
    T, D = x2.shape
    tm = MIX_ROWS
    d_attn = attn2.shape[1]
    wo = w_out.astype(BF16)
    full = lambda shape: pl.BlockSpec(shape, lambda i: (0,) * len(shape))
    rowblk = lambda w: pl.BlockSpec((tm, w), lambda i: (i, 0))
    colblk = pl.BlockSpec((TOP_K, tm), lambda i: (0, i))
    return pl.pallas_call(
        _mix_kernel,
        grid=(T // tm,),
        in_specs=[rowblk(D), rowblk(d_attn), rowblk(ssm2.shape[1]),
                  full((d_attn, D)), full((D - d_attn, D)), full((1, D)), full((1, D)),
                  full((N_EXPERTS, D)), full((N_EXPERTS, 1))],
        out_specs=[pl.BlockSpec((tm * SUBLANES, LANES), lambda i: (i, 0)),
                   colblk, colblk, full((N_EXPERTS, LANES))],
        out_shape=[jax.ShapeDtypeStruct((T * SUBLANES, LANES), F32),
                   jax.ShapeDtypeStruct((TOP_K, T), jnp.int32),
                   jax.ShapeDtypeStruct((TOP_K, T), F32),
                   jax.ShapeDtypeStruct((N_EXPERTS, LANES), F32)],
        scratch_shapes=[pltpu.VMEM((N_EXPERTS, LANES), F32)],
        compiler_params=_params("arbitrary"),
        name="mix_route",
    )(x2, attn2, ssm2, wo[:d_attn], wo[d_attn:], ln_g.reshape(1, D).astype(F32),
      ln_b.reshape(1, D).astype(F32), w_router.T.astype(BF16),
      b_router.reshape(N_EXPERTS, 1).astype(F32))


def _expert_kernel(be_ref, nu_ref, src_ref, cnt_ref, gsrc_ref, sdst_ref, x1_ref, wgu_ref, bg_ref,
                   bl_ref, wd_ref, bd_ref, o4_ref, sel_ref, wgl_ref, wdb_ref, xbuf, obuf, ibuf,
                   gsem, ssem, isem):
    i = pl.program_id(0)
    nu = nu_ref[0]
    _, d, f2 = wgu_ref.shape
    f = f2 // 2
    half = SPLIT_COLS // 2
    active = i < nu
    blk_rows = MOE_ROWS * SUBLANES

    def idx_copies(j):
        base = pl.multiple_of(lax.shift_right_logical(src_ref[j], IDX_SHIFT) * IDX_ALIGN, IDX_ALIGN)
        b = lax.rem(j, 3)
        dst = lambda h: ibuf.at[pl.ds(pl.multiple_of((2 * b + h) * IDX_WIN, IDX_WIN), IDX_WIN)]
        return (pltpu.make_async_copy(gsrc_ref.at[pl.ds(base, IDX_WIN)], dst(0), isem.at[b]),
                pltpu.make_async_copy(sdst_ref.at[pl.ds(base, IDX_WIN)], dst(1), isem.at[b]))

    def entry_base(j, h):
        return (2 * lax.rem(j, 3) + h) * IDX_WIN + (src_ref[j] & (IDX_ALIGN - 1))

    def rows(start, n=SUBLANES):
        if isinstance(start, int):
            return pl.ds(start, n)
        return pl.ds(pl.multiple_of(start, SUBLANES), n)

    def gather_row(slot, ebase, r):
        return pltpu.make_async_copy(x1_ref.at[rows(ibuf[ebase + r])],
                                     xbuf.at[rows(slot * blk_rows + r * SUBLANES)],
                                     gsem.at[slot])

    def scatter_row(slot, ebase, r):
        return pltpu.make_async_copy(obuf.at[rows(slot * blk_rows + r * SUBLANES)],
                                     o4_ref.at[rows(ibuf[ebase + r])], ssem.at[slot])

    def start_block(j, row_copy):
        for slot in range(2):
            @pl.when((j & 1) == slot)
            def _():
                for r in range(MOE_ROWS):
                    row_copy(slot, r).start(priority=r % 2)

    def start_gather(j):
        ebase = entry_base(j, 0)
        start_block(j, lambda slot, r: gather_row(slot, ebase, r))

    def wait_gather(j):
        pltpu.make_async_copy(x1_ref.at[rows(0, blk_rows)],
                              xbuf.at[rows((j & 1) * blk_rows, blk_rows)], gsem.at[j & 1]).wait()

    def start_scatter(j):
        ebase = entry_base(j, 1)

        @pl.when(cnt_ref[j] == MOE_ROWS)
        def _():
            start_block(j, lambda slot, r: scatter_row(slot, ebase, r))

        @pl.when(cnt_ref[j] < MOE_ROWS)
        def _():
            def one(r, c):
                scatter_row(j & 1, ebase, r).start()
                return c

            lax.fori_loop(0, cnt_ref[j], one, 0)

    def wait_scatter(j):
        @pl.when(cnt_ref[j] == MOE_ROWS)
        def _():
            pltpu.make_async_copy(obuf.at[rows((j & 1) * blk_rows, blk_rows)],
                                  o4_ref.at[rows(0, blk_rows)], ssem.at[j & 1]).wait()

        @pl.when(cnt_ref[j] < MOE_ROWS)
        def _():
            def one(r, c):
                pltpu.make_async_copy(obuf.at[rows(0)], o4_ref.at[rows(0)], ssem.at[j & 1]).wait()
                return c

            lax.fori_loop(0, cnt_ref[j], one, 0)

    @pl.when(i == 0)
    def _():
        r = lax.broadcasted_iota(jnp.int32, sel_ref.shape, 0)
        c = lax.broadcasted_iota(jnp.int32, sel_ref.shape, 1)
        src = jnp.where(c < half, 2 * c, 2 * (c - half) + 1)
        sel_ref[...] = (r == src).astype(BF16)
        for cp in idx_copies(0):
            cp.start()
        for cp in idx_copies(0):
            cp.wait()

        @pl.when(nu > 1)
        def _():
            for cp in idx_copies(1):
                cp.start()

        ebase0 = entry_base(0, 0)

        def first(r, c):
            gather_row(0, ebase0, r).start()
            return c

        lax.fori_loop(0, MOE_ROWS, first, 0)

    @pl.when(active)
    def _():
        @pl.when(i + 2 < nu)
        def _():
            for cp in idx_copies(i + 2):
                cp.start()

        wait_gather(i)

        @pl.when(i + 1 < nu)
        def _():
            for cp in idx_copies(i + 1):
                cp.wait()
            start_gather(i + 1)

    new_expert = (i == 0) | (be_ref[i] != be_ref[jnp.maximum(i - 1, 0)])

    @pl.when(active & new_expert)
    def _():
        def split_rows(rb, carry):
            rs = pl.ds(pl.multiple_of(rb * WPREP_ROWS, WPREP_ROWS), WPREP_ROWS)
            for j in range(f2 // SPLIT_COLS):
                w = wgu_ref[0, rs, j * SPLIT_COLS:(j + 1) * SPLIT_COLS].astype(BF16)
                y = jnp.dot(w, sel_ref[...], preferred_element_type=F32).astype(BF16)
                wgl_ref[rs, j * half:(j + 1) * half] = y[:, :half]
                wgl_ref[rs, f + j * half:f + (j + 1) * half] = y[:, half:]
            return carry

        def cast_rows(rb, carry):
            rs = pl.ds(pl.multiple_of(rb * WPREP_ROWS, WPREP_ROWS), WPREP_ROWS)
            wdb_ref[rs, :] = wd_ref[0, rs, :].astype(BF16)
            return carry

        lax.fori_loop(0, d // WPREP_ROWS, split_rows, 0)
        lax.fori_loop(0, f // WPREP_ROWS, cast_rows, 0)

    @pl.when(active)
    def _():
        base = pl.multiple_of((i & 1) * blk_rows, blk_rows)
        xb = _slab_load(xbuf.at[pl.ds(base, blk_rows)], MOE_ROWS, d).astype(BF16)
        h = jnp.dot(xb, wgl_ref[...], preferred_element_type=F32)
        xg = jnp.minimum(h[:, :f] + bg_ref[0], SWIGLU_LIMIT)
        xl = jnp.clip(h[:, f:] + bl_ref[0], -SWIGLU_LIMIT, SWIGLU_LIMIT)
        act = xg * jax.nn.sigmoid(SWIGLU_ALPHA * xg) * (xl + 1.0)
        out = jnp.dot(act.astype(BF16), wdb_ref[...], preferred_element_type=F32) + bd_ref[0]
        _slab_store(obuf.at[pl.ds(base, blk_rows)], out)

        start_scatter(i)

        @pl.when(i >= 1)
        def _():
            wait_scatter(i - 1)

        @pl.when(i == nu - 1)
        def _():
            wait_scatter(i)


def _experts(x1s, gsrc, sdst, block_e, n_used, blk_src, blk_cnt, w_gate_up, b_gate_up, w_down,
             b_down):
    E, D, F2 = w_gate_up.shape
    F = F2 // 2
    tm = MOE_ROWS
    n_blocks = block_e.shape[0]
    bg = b_gate_up[:, 0::2].reshape(E, 1, F).astype(F32)
    bl = b_gate_up[:, 1::2].reshape(E, 1, F).astype(F32)
    bd = b_down.reshape(E, 1, D).astype(F32)
    wmap = lambda i, be, nu, sr, cn: (be[i], 0, 0)
    any_spec = pl.BlockSpec(memory_space=pl.ANY)
    return pl.pallas_call(
        _expert_kernel,
        grid_spec=pltpu.PrefetchScalarGridSpec(
            num_scalar_prefetch=4,
            grid=(n_blocks,),
            in_specs=[any_spec, any_spec, any_spec,
                      pl.BlockSpec((1, D, F2), wmap),
                      pl.BlockSpec((1, 1, F), wmap), pl.BlockSpec((1, 1, F), wmap),
                      pl.BlockSpec((1, F, D), wmap), pl.BlockSpec((1, 1, D), wmap)],
            out_specs=any_spec,
            scratch_shapes=[pltpu.VMEM((SPLIT_COLS, SPLIT_COLS), BF16),
                            pltpu.VMEM((D, F2), BF16),
                            pltpu.VMEM((F, D), BF16),
                            pltpu.VMEM((2 * tm * SUBLANES, LANES), F32),
                            pltpu.VMEM((2 * tm * SUBLANES, LANES), F32),
                            pltpu.SMEM((3 * 2 * IDX_WIN,), jnp.int32),
                            pltpu.SemaphoreType.DMA((2,)), pltpu.SemaphoreType.DMA((2,)),
                            pltpu.SemaphoreType.DMA((3,))]),
        out_shape=jax.ShapeDtypeStruct((TOP_K * x1s.shape[0], LANES), F32),
        compiler_params=_params("arbitrary"),
        name="moe_experts",
    )(block_e, n_used, blk_src, blk_cnt, gsrc, sdst, x1s, w_gate_up, bg, bl, w_down, bd)


def _combine_kernel(gate_ref, x1_ref, o4_ref, p_ref, wpg_ref, wpp_ref, g_ref, b_ref, o_ref):
    tm, d = o_ref.shape
    pp = jnp.dot(p_ref[...].astype(BF16), wpp_ref[...], preferred_element_type=F32)
    gates = gate_ref[...]
    moe = gates[:, 0:1] * _slab_load(o4_ref, tm, d, lead=(0,))
    for k in range(1, TOP_K):
        moe = moe + gates[:, k:k + 1] * _slab_load(o4_ref, tm, d, lead=(k,))
    r = DEEPNORM_ALPHA * _slab_load(x1_ref, tm, d) + moe
    gate = jax.nn.sigmoid(jnp.dot(r.astype(BF16), wpg_ref[...], preferred_element_type=F32))
    o_ref[...] = _layer_norm(r + gate * pp, g_ref[...], b_ref[...])


def _combine(gates_t, x1s, out4, p2, w_ple_gate, w_ple_proj, ln_g, ln_b):
    T, pd = p2.shape
    D = w_ple_gate.shape[0]
    tm = COMBINE_ROWS
    full = lambda shape: pl.BlockSpec(shape, lambda i: (0,) * len(shape))
    return pl.pallas_call(
        _combine_kernel,
        grid=(T // tm,),
        in_specs=[pl.BlockSpec((tm, TOP_K), lambda i: (i, 0)),
                  pl.BlockSpec((tm * SUBLANES, LANES), lambda i: (i, 0)),
                  pl.BlockSpec((TOP_K, tm * SUBLANES, LANES), lambda i: (0, i, 0)),
                  pl.BlockSpec((tm, pd), lambda i: (i, 0)),
                  full((D, D)), full((pd, D)), full((1, D)), full((1, D))],
        out_specs=pl.BlockSpec((tm, D), lambda i: (i, 0)),
        out_shape=jax.ShapeDtypeStruct((T, D), F32),
        compiler_params=_params("parallel"),
        name="moe_combine",
    )(gates_t, x1s, out4, p2, w_ple_gate.astype(BF16), w_ple_proj.astype(BF16),
      ln_g.reshape(1, D).astype(F32), ln_b.reshape(1, D).astype(F32))


def kernel(x, p, w_in, lambda_q1, lambda_k1, lambda_q2, lambda_k2, subln_g, ssm_a_re, ssm_a_im,
           ssm_log_dt, ssm_b_re, ssm_b_im, ssm_c_re, ssm_c_im, ssm_d, w_glu, ssm_norm_g, w_out,
           ln1_g, ln1_b, w_router, b_router, w_gate_up, b_gate_up, w_down, b_down, w_ple_gate,
           w_ple_proj, ln2_g, ln2_b):
    B, L, D = x.shape
    T = B * L
    assert D == SUBLANES * LANES, "token-slab layout holds one token per (8, 128) tile"
    for i in range(DEPTH):
        lambda_init = 0.8 - 0.6 * math.exp(-0.3 * i)
        q, kt, v, u = _in_proj(x, w_in[i])
        lam_vecs = jnp.stack([lambda_q1[i], lambda_k1[i], lambda_q2[i], lambda_k2[i]]).astype(F32)
        attn = _attention(q, kt, v, lam_vecs, subln_g[i], lambda_init)
        b_fold, c_fold, a_tile = _ssm_fold_params(ssm_a_re[i], ssm_a_im[i], ssm_log_dt[i],
                                                  ssm_b_re[i], ssm_b_im[i], ssm_c_re[i],
                                                  ssm_c_im[i], B)
        ssm = _ssm(u, b_fold, c_fold, a_tile, ssm_d[i], w_glu[i], ssm_norm_g[i])

        x1, idx, gates, counts = _mix_route(
            x.reshape(T, D), attn.reshape(T, -1), ssm.reshape(T, -1), w_out[i], ln1_g[i],
            ln1_b[i], w_router[i], b_router[i])

        order = jnp.argsort(idx.T.reshape(-1), stable=True).astype(jnp.int32)
        order = jnp.concatenate([order, jnp.zeros((IDX_WIN,), jnp.int32)])
        tok_row = lax.shift_right_logical(order, TOP_K_SHIFT) * SUBLANES
        gsrc = tok_row
        sdst = (order & (TOP_K - 1)) * (T * SUBLANES) + tok_row
        cnt = counts[:, 0].astype(jnp.int32)
        start = jnp.cumsum(cnt) - cnt
        n_blk_e = (cnt + MOE_ROWS - 1) // MOE_ROWS
        blk_end = jnp.cumsum(n_blk_e)
        n_blocks = (T * TOP_K) // MOE_ROWS + N_EXPERTS
        blk = jnp.arange(n_blocks, dtype=jnp.int32)
        block_e = jnp.minimum(jnp.sum(blk_end[None, :] <= blk[:, None], axis=1),
                              N_EXPERTS - 1).astype(jnp.int32)
        local = blk - (blk_end - n_blk_e)[block_e]
        blk_src = (start[block_e] + local * MOE_ROWS).astype(jnp.int32)
        blk_cnt = jnp.clip(cnt[block_e] - local * MOE_ROWS, 0, MOE_ROWS).astype(jnp.int32)
        n_used = blk_end[-1:].astype(jnp.int32)
        blk_src = jnp.where(blk < n_used, blk_src, 0)
        blk_cnt = jnp.where(blk < n_used, blk_cnt, 0)

        out4 = _experts(x1, gsrc, sdst, block_e, n_used, blk_src, blk_cnt, w_gate_up[i],
                        b_gate_up[i], w_down[i], b_down[i])
        x = _combine(gates.T, x1, out4.reshape(TOP_K, T * SUBLANES, LANES), p[i].reshape(T, -1), w_ple_gate[i], w_ple_proj[i],
                     ln2_g[i], ln2_b[i]).reshape(B, L, D)
    return x
```

```python
import functools
import math

import jax
import jax.numpy as jnp
from jax import lax
from jax.experimental import pallas as pl
from jax.experimental.pallas import tpu as pltpu

F32 = jnp.float32
BF16 = jnp.bfloat16

N_HEADS = 8
HEAD_DIM = 32
V_DIM = 2 * HEAD_DIM
D_ATTN = N_HEADS * V_DIM
SSM_GROUP = 16
SSM_STATE = 64
N_EXPERTS = 32
TOP_K = 4
TOP_K_SHIFT = 2
SWIGLU_LIMIT = 7.0
SWIGLU_ALPHA = 1.702
LN_EPS = 1e-5
RMS_EPS = 1e-5
DEPTH = 1
DEEPNORM_ALPHA = (2.0 * DEPTH) ** 0.25
LOG2E = math.log2(math.e)

LANES = 128
SUBLANES = 8
VMEM_LIMIT = 56 * 1024 * 1024

PROJ_ROWS = 512
ATTN_Q = 512
BIAS_PARTS = 3
SSM_CHUNK = 128
SSM_MM_ROWS = 256
MIX_ROWS = 512
MOE_ROWS = 256
SPLIT_COLS = 512
WPREP_ROWS = 256
IDX_WIN = 2048
IDX_ALIGN = 1024
IDX_SHIFT = 10
COMBINE_ROWS = 256


def _params(*sem):
    return pltpu.CompilerParams(dimension_semantics=sem, vmem_limit_bytes=VMEM_LIMIT)


def _in_proj_kernel(x_ref, wq_ref, wkt_ref, wv_ref, wu_ref, q_ref, kt_ref, v_ref, u_ref):
    xb = x_ref[0].astype(BF16)
    q = jnp.dot(xb, wq_ref[...], preferred_element_type=F32)
    q_ref[0] = (q * (HEAD_DIM ** -0.5 * LOG2E)).astype(BF16)
    kt = lax.dot_general(wkt_ref[...], xb, (((1,), (1,)), ((), ())),
                         preferred_element_type=F32)
    kt_ref[0, 0] = kt.astype(BF16)
    lane_v = lax.broadcasted_iota(jnp.int32, (1, wv_ref.shape[1]), 1)
    ones_lane = (lane_v % LANES == V_DIM).astype(F32)
    v_ref[0] = (jnp.dot(xb, wv_ref[...], preferred_element_type=F32) + ones_lane).astype(BF16)
    u_ref[0] = jnp.dot(xb, wu_ref[...], preferred_element_type=F32)


def _in_proj(x, w_in):
    B, L, D = x.shape
    d_ssm = w_in.shape[1] - 3 * D_ATTN
    tm = PROJ_ROWS
    nt = L // tm
    wb = w_in.astype(BF16)
    wq = wb[:, :D_ATTN]
    wkt = wb[:, D_ATTN:2 * D_ATTN].T
    wv = wb[:, 2 * D_ATTN:3 * D_ATTN].reshape(D, N_HEADS, V_DIM)
    wv = jnp.pad(wv, ((0, 0), (0, 0), (0, LANES - V_DIM))).reshape(D, N_HEADS * LANES)
    wu = wb[:, 3 * D_ATTN:]
    full = lambda shape: pl.BlockSpec(shape, lambda b, i: (0,) * len(shape))
    return pl.pallas_call(
        _in_proj_kernel,
        grid=(B, nt),
        in_specs=[pl.BlockSpec((1, tm, D), lambda b, i: (b, i, 0)),
                  full((D, D_ATTN)), full((D_ATTN, D)), full((D, N_HEADS * LANES)),
                  full((D, d_ssm))],
        out_specs=[pl.BlockSpec((1, tm, D_ATTN), lambda b, i: (b, i, 0)),
                   pl.BlockSpec((1, 1, D_ATTN, tm), lambda b, i: (b, i, 0, 0)),
                   pl.BlockSpec((1, tm, N_HEADS * LANES), lambda b, i: (b, i, 0)),
                   pl.BlockSpec((1, tm, d_ssm), lambda b, i: (b, i, 0))],
        out_shape=[jax.ShapeDtypeStruct((B, L, D_ATTN), BF16),
                   jax.ShapeDtypeStruct((B, nt, D_ATTN, tm), BF16),
                   jax.ShapeDtypeStruct((B, L, N_HEADS * LANES), BF16),
                   jax.ShapeDtypeStruct((B, L, d_ssm), F32)],
        compiler_params=_params("parallel", "parallel"),
        name="in_proj",
    )(x, wq, wkt, wv, wu)


def _attn_kernel(slope_ref, lam_ref, q_ref, kt_ref, v_ref, g_ref, o_ref,
                 q4_ref, jr_ref, p_ref, m_ref, acc_ref, *, tq, tk, lambda_init):
    hp = pl.program_id(1)
    qi = pl.program_id(2)
    q0 = qi * tq
    n_cb = tk // LANES

    q = q_ref[0]
    lane_q = lax.broadcasted_iota(jnp.int32, q.shape, 1)
    for c in range(4):
        rs = slice(c * tq, (c + 1) * tq)
        lo = (c // 2) * BIAS_PARTS
        q4_ref[rs, :LANES] = jnp.where(lane_q // HEAD_DIM == c, q, jnp.zeros_like(q))
        q4_ref[rs, LANES:] = ((lane_q >= lo) & (lane_q < lo + BIAS_PARTS)).astype(BF16)

    @pl.when(qi == 0)
    def _():
        colf = lax.broadcasted_iota(jnp.int32, (1, tk), 1).astype(F32)
        r_i = lax.broadcasted_iota(jnp.int32, (LANES, tk), 0)
        jr = jnp.zeros((LANES, tk), F32)
        for h in range(2):
            rem = (slope_ref[2 * hp + h] * LOG2E) * colf
            for part in range(BIAS_PARTS):
                piece = rem.astype(BF16).astype(F32)
                jr = jnp.where(r_i == h * BIAS_PARTS + part, piece, jr)
                rem = rem - piece
        jr_ref[...] = jr.astype(BF16)

    m_ref[...] = jnp.full(m_ref.shape, -jnp.inf, F32)
    acc_ref[...] = jnp.zeros(acc_ref.shape, F32)

    col = lax.broadcasted_iota(jnp.int32, (1, tk), 1)
    row = lax.broadcasted_iota(jnp.int32, (tq, 1), 0)
    zero_row = jnp.zeros((1, LANES), jnp.int32)

    def v_tile(t):
        return v_ref[0, pl.ds(pl.multiple_of(t * tk, tk), tk), :]

    def tile(t, masked):
        kt_aug = jnp.concatenate([kt_ref[0, t], jr_ref[...]], axis=0)
        s_all = jnp.dot(q4_ref[...], kt_aug, preferred_element_type=F32)
        rel = t * tk - q0 + col
        base = (t * tk - q0 + zero_row).astype(F32)
        for c in range(4):
            rs = slice(c * tq, (c + 1) * tq)
            off = (slope_ref[2 * hp + c // 2] * LOG2E) * base
            s = s_all[rs]
            if masked:
                s = jnp.where(rel <= row, s, -jnp.inf)
            blocks = [s[:, cb * LANES:(cb + 1) * LANES] for cb in range(n_cb)]
            part = blocks[0]
            for blk in blocks[1:]:
                part = jnp.maximum(part, blk)
            m_old = m_ref[rs]
            m_new = jnp.maximum(m_old, jnp.max(part, axis=1, keepdims=True) + off)
            m_ref[rs] = m_new
            alpha = jnp.exp2(m_old - m_new)
            shift = m_new - off
            acc_ref[rs] = alpha * acc_ref[rs]
            p_ref[rs] = jnp.concatenate([jnp.exp2((blk - shift).astype(BF16)) for blk in blocks],
                                        axis=1)
        v2 = v_tile(t)
        for j in range(2):
            hr = slice(2 * j * tq, (2 * j + 2) * tq)
            acc_ref[hr] += jnp.dot(p_ref[hr], v2[:, j * LANES:(j + 1) * LANES],
                                   preferred_element_type=F32)

    n_full = q0 // tk

    def body(t, carry):
        tile(t, False)
        return carry

    lax.fori_loop(0, n_full, body, 0)
    tile(n_full, True)
    acc = acc_ref[...]

    lam = lam_ref[...]
    lane = lax.broadcasted_iota(jnp.int32, (1, LANES), 1)
    outs = []
    for j in range(2):
        o = []
        for c in (2 * j, 2 * j + 1):
            a = acc[c * tq:(c + 1) * tq]
            denom = jnp.sum(jnp.where(lane == V_DIM, a, 0.0), axis=1, keepdims=True)
            o.append(a / denom)
        d = o[0] - lam * o[1]
        ms = jnp.sum(jnp.where(lane < V_DIM, d * d, 0.0), axis=1, keepdims=True) * (1.0 / V_DIM)
        outs.append(d * lax.rsqrt(ms + RMS_EPS))
    out = jnp.where(lane < V_DIM, outs[0], pltpu.roll(outs[1], V_DIM, 1))
    out = out * g_ref[...] * (1.0 - lambda_init)
    o_ref[0] = out.astype(o_ref.dtype)


def _attention(q, kt, v, lam_vecs, subln_g, lambda_init):
    B, L, _ = q.shape
    tq, tk = ATTN_Q, PROJ_ROWS
    nk = L // tk
    slopes = jnp.exp2(-(jnp.arange(N_HEADS, dtype=F32) + 1.0) * (8.0 / N_HEADS))
    lam = (jnp.exp(jnp.sum(lam_vecs[0] * lam_vecs[1])) - jnp.exp(jnp.sum(lam_vecs[2] * lam_vecs[3]))
           + lambda_init).reshape(1, 1).astype(F32)
    g2 = jnp.tile(subln_g.astype(F32), 2).reshape(1, LANES)
    kern = functools.partial(_attn_kernel, tq=tq, tk=tk, lambda_init=lambda_init)
    return pl.pallas_call(
        kern,
        grid_spec=pltpu.PrefetchScalarGridSpec(
            num_scalar_prefetch=1,
            grid=(B, N_HEADS // 2, L // tq),
            in_specs=[pl.BlockSpec((1, 1), lambda b, h, i, s: (0, 0)),
                      pl.BlockSpec((1, tq, LANES), lambda b, h, i, s: (b, i, h)),
                      pl.BlockSpec((1, nk, LANES, tk), lambda b, h, i, s: (b, 0, h, 0)),
                      pl.BlockSpec((1, L, 2 * LANES), lambda b, h, i, s: (b, 0, h)),
                      pl.BlockSpec((1, LANES), lambda b, h, i, s: (0, 0))],
            out_specs=pl.BlockSpec((1, tq, LANES), lambda b, h, i, s: (b, i, h)),
            scratch_shapes=[pltpu.VMEM((4 * tq, 2 * LANES), BF16),
                            pltpu.VMEM((LANES, tk), BF16),
                            pltpu.VMEM((4 * tq, tk), BF16),
                            pltpu.VMEM((4 * tq, LANES), F32),
                            pltpu.VMEM((4 * tq, LANES), F32)]),
        out_shape=jax.ShapeDtypeStruct((B, L, D_ATTN), BF16),
        compiler_params=_params("parallel", "parallel", "arbitrary"),
        name="diff_attn",
    )(slopes, lam, q, kt, v, g2)


def _ssm_fold_params(a_re, a_im, log_dt, b_re, b_im, c_re, c_im, batch):
    G, P = a_re.shape
    C = b_re.shape[-1]
    half = G // 2
    A = lax.complex(a_re.astype(F32), a_im.astype(F32))
    dt = jnp.exp(log_dt.astype(F32))[:, None]
    a_bar = jnp.exp(A * dt)
    b_bar = ((a_bar - 1.0) / A)[..., None] * lax.complex(b_re.astype(F32), b_im.astype(F32))
    sel = jnp.eye(half, dtype=F32)[jnp.arange(G) % half]
    fold_b = lambda m: jnp.einsum('gpc,gk->gckp', m, sel).reshape(G * C, half * P)
    b_fold = jnp.concatenate([fold_b(jnp.real(b_bar)), fold_b(jnp.imag(b_bar))], axis=1)
    fold_c = lambda m: jnp.einsum('gcp,gk->kpgc', m, sel).reshape(half * P, G * C)
    c_fold = jnp.concatenate([fold_c(c_re.astype(F32)), -fold_c(c_im.astype(F32))], axis=0)
    a_rows = jnp.tile(a_bar.reshape(2, half * P), (batch, 1))
    a_tile = jnp.concatenate([jnp.real(a_rows), jnp.imag(a_rows)], axis=1)
    return b_fold.astype(BF16), c_fold.astype(BF16), a_tile.astype(F32)


def _gelu_tanh(x):
    return 0.5 * x * (1.0 + jnp.tanh(math.sqrt(2.0 / math.pi) * (x + 0.044715 * (x * x * x))))


def _ssm_kernel(u_ref, bf_ref, cf_ref, a_ref, d_ref, wglu_ref, gn_ref, o_ref,
                lhs_ref, x_ref, y_ref, s_ref, *, chunk, batch):
    i = pl.program_id(0)
    d_ssm = u_ref.shape[-1]
    n_slab = d_ssm // LANES
    ns = x_ref.shape[-1] // 2
    rows = SUBLANES

    @pl.when(i == 0)
    def _():
        s_ref[...] = jnp.zeros_like(s_ref)

    zero = jnp.zeros((chunk, LANES), F32)
    for b in range(batch):
        ub = u_ref[b]
        for h in range(2):
            for c in range(n_slab):
                src = ub[:, c * LANES:(c + 1) * LANES] if (c * 2) // n_slab == h else zero
                lhs_ref[c, pl.ds(2 * b + h, chunk, stride=rows), :] = src
    n_blk = (rows * chunk) // SSM_MM_ROWS

    hs = ns // 2
    n_set = 2

    def in_mm(rb, carry):
        rs = pl.ds(pl.multiple_of(rb * SSM_MM_ROWS, SSM_MM_ROWS), SSM_MM_ROWS)
        for s in range(n_set):
            lhs = jnp.concatenate([lhs_ref[s, rs, :], lhs_ref[s + n_slab // 2, rs, :]],
                                  axis=1).astype(BF16)
            xs = jnp.dot(lhs, bf_ref[s], preferred_element_type=F32)
            x_ref[rs, s * hs:(s + 1) * hs] = xs[:, :hs]
            x_ref[rs, ns + s * hs:ns + (s + 1) * hs] = xs[:, hs:]
        return carry

    lax.fori_loop(0, n_blk, in_mm, 0)

    a_re = a_ref[:, :ns]
    a_im = a_ref[:, ns:]

    def step(t, carry):
        s_re, s_im = carry
        r0 = pl.multiple_of(t * rows, rows)
        n_re = a_re * s_re - a_im * s_im + x_ref[pl.ds(r0, rows), :ns]
        n_im = a_re * s_im + a_im * s_re + x_ref[pl.ds(r0, rows), ns:]
        x_ref[pl.ds(r0, rows), :ns] = n_re
        x_ref[pl.ds(r0, rows), ns:] = n_im
        return n_re, n_im

    s_re, s_im = lax.fori_loop(0, chunk, step, (s_ref[:, :ns], s_ref[:, ns:]), unroll=2)
    s_ref[:, :ns] = s_re
    s_ref[:, ns:] = s_im

    def out_mm(rb, carry):
        rs = pl.ds(pl.multiple_of(rb * SSM_MM_ROWS, SSM_MM_ROWS), SSM_MM_ROWS)
        for s in range(n_set):
            st = jnp.concatenate([x_ref[rs, s * hs:(s + 1) * hs],
                                  x_ref[rs, ns + s * hs:ns + (s + 1) * hs]], axis=1).astype(BF16)
            y = jnp.dot(st, cf_ref[s], preferred_element_type=F32)
            y_ref[s, rs, :] = y[:, :LANES]
            y_ref[s + n_slab // 2, rs, :] = y[:, LANES:]
        return carry

    lax.fori_loop(0, n_blk, out_mm, 0)

    for b in range(batch):
        parts = []
        for c in range(n_slab):
            h = (c * 2) // n_slab
            parts.append(y_ref[c, pl.ds(2 * b + h, chunk, stride=rows), :])
        yb = jnp.concatenate(parts, axis=1) + d_ref[...] * u_ref[b]
        g = jnp.dot(_gelu_tanh(yb).astype(BF16), wglu_ref[...], preferred_element_type=F32)
        z = g[:, :d_ssm] * jax.nn.sigmoid(g[:, d_ssm:])
        ms = jnp.mean(z * z, axis=1, keepdims=True)
        o_ref[b] = (z * lax.rsqrt(ms + RMS_EPS) * gn_ref[...]).astype(o_ref.dtype)


def _ssm(u, b_fold, c_fold, a_tile, d_skip, w_glu, norm_g):
    B, L, d_ssm = u.shape
    assert 2 * B == SUBLANES, "row-stream layout packs batch x 2 group halves on 8 sublanes"
    chunk = SSM_CHUNK
    ns2 = a_tile.shape[1]
    n_slab = d_ssm // LANES
    assert n_slab == 4, "two group sets x two group halves of 128 channels"
    ns, hs = ns2 // 2, ns2 // 4
    ch = lambda s: jnp.r_[s * LANES:(s + 1) * LANES, (s + 2) * LANES:(s + 3) * LANES]
    st = lambda s: jnp.r_[s * hs:(s + 1) * hs, ns + s * hs:ns + (s + 1) * hs]
    b_sets = jnp.stack([b_fold[ch(s)][:, st(s)] for s in range(2)])
    c_sets = jnp.stack([c_fold[st(s)][:, ch(s)] for s in range(2)])
    full = lambda shape: pl.BlockSpec(shape, lambda i: (0,) * len(shape))
    kern = functools.partial(_ssm_kernel, chunk=chunk, batch=B)
    return pl.pallas_call(
        kern,
        grid=(L // chunk,),
        in_specs=[pl.BlockSpec((B, chunk, d_ssm), lambda i: (0, i, 0)),
                  full(b_sets.shape), full(c_sets.shape), full((SUBLANES, ns2)),
                  full((1, d_ssm)), full((d_ssm, 2 * d_ssm)), full((1, d_ssm))],
        out_specs=pl.BlockSpec((B, chunk, d_ssm), lambda i: (0, i, 0)),
        out_shape=jax.ShapeDtypeStruct((B, L, d_ssm), BF16),
        scratch_shapes=[pltpu.VMEM((n_slab, SUBLANES * chunk, LANES), F32),
                        pltpu.VMEM((SUBLANES * chunk, ns2), F32),
                        pltpu.VMEM((n_slab, SUBLANES * chunk, LANES), F32),
                        pltpu.VMEM((SUBLANES, ns2), F32)],
        compiler_params=_params("arbitrary"),
        name="s5_ssm",
    )(u, b_sets, c_sets, a_tile, d_skip.reshape(1, d_ssm).astype(F32), w_glu.astype(BF16),
      norm_g.reshape(1, d_ssm).astype(F32))


def _layer_norm(x, g, b):
    mu = jnp.mean(x, axis=1, keepdims=True)
    xc = x - mu
    var = jnp.mean(xc * xc, axis=1, keepdims=True)
    return xc * lax.rsqrt(var + LN_EPS) * g + b


def _slab_load(ref, n_rows, d, lead=()):
    return jnp.concatenate(
        [ref[lead + (pl.ds(c, n_rows, stride=SUBLANES), slice(None))] for c in range(d // LANES)],
        axis=1)


def _slab_store(ref, val):
    n_rows, d = val.shape
    for c in range(d // LANES):
        ref[pl.ds(c, n_rows, stride=SUBLANES), :] = val[:, c * LANES:(c + 1) * LANES]


def _mix_kernel(x_ref, attn_ref, ssm_ref, woa_ref, wos_ref, g_ref, b_ref, wrt_ref, br_ref,
                x1_ref, idx_ref, gate_ref, cnt_ref, carry_ref):
    i = pl.program_id(0)
    tm = x_ref.shape[0]

    @pl.when(i == 0)
    def _():
        carry_ref[...] = jnp.zeros_like(carry_ref)

    mix = (jnp.dot(attn_ref[...], woa_ref[...], preferred_element_type=F32)
           + jnp.dot(ssm_ref[...], wos_ref[...], preferred_element_type=F32))
    x1 = _layer_norm(DEEPNORM_ALPHA * x_ref[...] + mix, g_ref[...], b_ref[...])
    _slab_store(x1_ref, x1)

    logits = lax.dot_general(wrt_ref[...], x1.astype(BF16), (((1,), (1,)), ((), ())),
                             preferred_element_type=F32) + br_ref[...]
    e_iota = lax.broadcasted_iota(jnp.int32, logits.shape, 0).astype(F32)
    work = logits
    vals, hots = [], []
    for k in range(TOP_K):
        mx = jnp.max(work, axis=0, keepdims=True)
        idx = jnp.min(jnp.where(work == mx, e_iota, float(N_EXPERTS)), axis=0, keepdims=True)
        hot = e_iota == idx
        work = jnp.where(hot, -jnp.inf, work)
        vals.append(mx)
        hots.append(hot)
        idx_ref[k:k + 1, :] = idx.astype(jnp.int32)
    ex = [jnp.exp(v - vals[0]) for v in vals]
    den = ex[0] + ex[1] + ex[2] + ex[3]
    for k in range(TOP_K):
        gate_ref[k:k + 1, :] = ex[k] / den

    sel = (hots[0] | hots[1] | hots[2] | hots[3]).astype(F32)
    carry_ref[...] = carry_ref[...] + jnp.sum(sel, axis=1, keepdims=True)
    cnt_ref[...] = carry_ref[...]


def _mix_route(x2, attn2, ssm2, w_out, ln_g, ln_b, w_router, b_router):
    T, D = x2.shape
    tm = MIX_ROWS
    d_attn = attn2.shape[1]
    wo = w_out.astype(BF16)
    full = lambda shape: pl.BlockSpec(shape, lambda i: (0,) * len(shape))
    rowblk = lambda w: pl.BlockSpec((tm, w), lambda i: (i, 0))
    colblk = pl.BlockSpec((TOP_K, tm), lambda i: (0, i))
    return pl.pallas_call(
        _mix_kernel,
        grid=(T // tm,),
        in_specs=[rowblk(D), rowblk(d_attn), rowblk(ssm2.shape[1]),
                  full((d_attn, D)), full((D - d_attn, D)), full((1, D)), full((1, D)),
                  full((N_EXPERTS, D)), full((N_EXPERTS, 1))],
        out_specs=[pl.BlockSpec((tm * SUBLANES, LANES), lambda i: (i, 0)),
                   colblk, colblk, full((N_EXPERTS, LANES))],
        out_shape=[jax.ShapeDtypeStruct((T * SUBLANES, LANES), F32),
                   jax.ShapeDtypeStruct((TOP_K, T), jnp.int32),
                   jax.ShapeDtypeStruct((TOP_K, T), F32),
                   jax.ShapeDtypeStruct((N_EXPERTS, LANES), F32)],
        scratch_shapes=[pltpu.VMEM((N_EXPERTS, LANES), F32)],
        compiler_params=_params("arbitrary"),
        name="mix_route",
    )(x2, attn2, ssm2, wo[:d_attn], wo[d_attn:], ln_g.reshape(1, D).astype(F32),
      ln_b.reshape(1, D).astype(F32), w_router.T.astype(BF16),
      b_router.reshape(N_EXPERTS, 1).astype(F32))


def _expert_kernel(be_ref, nu_ref, src_ref, cnt_ref, gsrc_ref, sdst_ref, x1_ref, wgu_ref, bg_ref,
                   bl_ref, wd_ref, bd_ref, o4_ref, sel_ref, wgl_ref, wdb_ref, xbuf, obuf, ibuf,
                   gsem, ssem, isem):
    i = pl.program_id(0)
    nu = nu_ref[0]
    _, d, f2 = wgu_ref.shape
    f = f2 // 2
    half = SPLIT_COLS // 2
    active = i < nu
    blk_rows = MOE_ROWS * SUBLANES

    def idx_copies(j):
        base = pl.multiple_of(lax.shift_right_logical(src_ref[j], IDX_SHIFT) * IDX_ALIGN, IDX_ALIGN)
        b = lax.rem(j, 3)
        dst = lambda h: ibuf.at[pl.ds(pl.multiple_of((2 * b + h) * IDX_WIN, IDX_WIN), IDX_WIN)]
        return (pltpu.make_async_copy(gsrc_ref.at[pl.ds(base, IDX_WIN)], dst(0), isem.at[b]),
                pltpu.make_async_copy(sdst_ref.at[pl.ds(base, IDX_WIN)], dst(1), isem.at[b]))

    def entry_base(j, h):
        return (2 * lax.rem(j, 3) + h) * IDX_WIN + (src_ref[j] & (IDX_ALIGN - 1))

    def rows(start, n=SUBLANES):
        if isinstance(start, int):
            return pl.ds(start, n)
        return pl.ds(pl.multiple_of(start, SUBLANES), n)

    def gather_row(slot, ebase, r):
        return pltpu.make_async_copy(x1_ref.at[rows(ibuf[ebase + r])],
                                     xbuf.at[rows(slot * blk_rows + r * SUBLANES)],
                                     gsem.at[slot])

    def scatter_row(slot, ebase, r):
        return pltpu.make_async_copy(obuf.at[rows(slot * blk_rows + r * SUBLANES)],
                                     o4_ref.at[rows(ibuf[ebase + r])], ssem.at[slot])

    def start_block(j, row_copy):
        for slot in range(2):
            @pl.when((j & 1) == slot)
            def _():
                for r in range(MOE_ROWS):
                    row_copy(slot, r).start(priority=r % 2)

    def start_gather(j):
        ebase = entry_base(j, 0)
        start_block(j, lambda slot, r: gather_row(slot, ebase, r))

    def wait_gather(j):
        pltpu.make_async_copy(x1_ref.at[rows(0, blk_rows)],
                              xbuf.at[rows((j & 1) * blk_rows, blk_rows)], gsem.at[j & 1]).wait()

    def start_scatter(j):
        ebase = entry_base(j, 1)

        @pl.when(cnt_ref[j] == MOE_ROWS)
        def _():
            start_block(j, lambda slot, r: scatter_row(slot, ebase, r))

        @pl.when(cnt_ref[j] < MOE_ROWS)
        def _():
            def one(r, c):
                scatter_row(j & 1, ebase, r).start()
                return c

            lax.fori_loop(0, cnt_ref[j], one, 0)

    def wait_scatter(j):
        @pl.when(cnt_ref[j] == MOE_ROWS)
        def _():
            pltpu.make_async_copy(obuf.at[rows((j & 1) * blk_rows, blk_rows)],
                                  o4_ref.at[rows(0, blk_rows)], ssem.at[j & 1]).wait()

        @pl.when(cnt_ref[j] < MOE_ROWS)
        def _():
            def one(r, c):
                pltpu.make_async_copy(obuf.at[rows(0)], o4_ref.at[rows(0)], ssem.at[j & 1]).wait()
                return c

            lax.fori_loop(0, cnt_ref[j], one, 0)

    @pl.when(i == 0)
    def _():
        r = lax.broadcasted_iota(jnp.int32, sel_ref.shape, 0)
        c = lax.broadcasted_iota(jnp.int32, sel_ref.shape, 1)
        src = jnp.where(c < half, 2 * c, 2 * (c - half) + 1)
        sel_ref[...] = (r == src).astype(BF16)
        for cp in idx_copies(0):
            cp.start()
        for cp in idx_copies(0):
            cp.wait()

        @pl.when(nu > 1)
        def _():
            for cp in idx_copies(1):
                cp.start()

        ebase0 = entry_base(0, 0)

        def first(r, c):
            gather_row(0, ebase0, r).start()
            return c

        lax.fori_loop(0, MOE_ROWS, first, 0)

    @pl.when(active)
    def _():
        @pl.when(i + 2 < nu)
        def _():
            for cp in idx_copies(i + 2):
                cp.start()

        wait_gather(i)

        @pl.when(i + 1 < nu)
        def _():
            for cp in idx_copies(i + 1):
                cp.wait()
            start_gather(i + 1)

    new_expert = (i == 0) | (be_ref[i] != be_ref[jnp.maximum(i - 1, 0)])

    @pl.when(active & new_expert)
    def _():
        def split_rows(rb, carry):
            rs = pl.ds(pl.multiple_of(rb * WPREP_ROWS, WPREP_ROWS), WPREP_ROWS)
            for j in range(f2 // SPLIT_COLS):
                w = wgu_ref[0, rs, j * SPLIT_COLS:(j + 1) * SPLIT_COLS].astype(BF16)
                y = jnp.dot(w, sel_ref[...], preferred_element_type=F32).astype(BF16)
                wgl_ref[rs, j * half:(j + 1) * half] = y[:, :half]
                wgl_ref[rs, f + j * half:f + (j + 1) * half] = y[:, half:]
            return carry

        def cast_rows(rb, carry):
            rs = pl.ds(pl.multiple_of(rb * WPREP_ROWS, WPREP_ROWS), WPREP_ROWS)
            wdb_ref[rs, :] = wd_ref[0, rs, :].astype(BF16)
            return carry

        lax.fori_loop(0, d // WPREP_ROWS, split_rows, 0)
        lax.fori_loop(0, f // WPREP_ROWS, cast_rows, 0)

    @pl.when(active)
    def _():
        base = pl.multiple_of((i & 1) * blk_rows, blk_rows)
        xb = _slab_load(xbuf.at[pl.ds(base, blk_rows)], MOE_ROWS, d).astype(BF16)
        h = jnp.dot(xb, wgl_ref[...], preferred_element_type=F32)
        xg = jnp.minimum(h[:, :f] + bg_ref[0], SWIGLU_LIMIT)
        xl = jnp.clip(h[:, f:] + bl_ref[0], -SWIGLU_LIMIT, SWIGLU_LIMIT)
        act = xg * jax.nn.sigmoid(SWIGLU_ALPHA * xg) * (xl + 1.0)
        out = jnp.dot(act.astype(BF16), wdb_ref[...], preferred_element_type=F32) + bd_ref[0]
        _slab_store(obuf.at[pl.ds(base, blk_rows)], out)

        start_scatter(i)

        @pl.when(i >= 1)
        def _():
            wait_scatter(i - 1)

        @pl.when(i == nu - 1)
        def _():
            wait_scatter(i)


def _experts(x1s, gsrc, sdst, block_e, n_used, blk_src, blk_cnt, w_gate_up, b_gate_up, w_down,
             b_down):
    E, D, F2 = w_gate_up.shape
    F = F2 // 2
    tm = MOE_ROWS
    n_blocks = block_e.shape[0]
    bg = b_gate_up[:, 0::2].reshape(E, 1, F).astype(F32)
    bl = b_gate_up[:, 1::2].reshape(E, 1, F).astype(F32)
    bd = b_down.reshape(E, 1, D).astype(F32)
    wmap = lambda i, be, nu, sr, cn: (be[i], 0, 0)
    any_spec = pl.BlockSpec(memory_space=pl.ANY)
    return pl.pallas_call(
        _expert_kernel,
        grid_spec=pltpu.PrefetchScalarGridSpec(
            num_scalar_prefetch=4,
            grid=(n_blocks,),
            in_specs=[any_spec, any_spec, any_spec,
                      pl.BlockSpec((1, D, F2), wmap),
                      pl.BlockSpec((1, 1, F), wmap), pl.BlockSpec((1, 1, F), wmap),
                      pl.BlockSpec((1, F, D), wmap), pl.BlockSpec((1, 1, D), wmap)],
            out_specs=any_spec,
            scratch_shapes=[pltpu.VMEM((SPLIT_COLS, SPLIT_COLS), BF16),
                            pltpu.VMEM((D, F2), BF16),
                            pltpu.VMEM((F, D), BF16),
                            pltpu.VMEM((2 * tm * SUBLANES, LANES), F32),
                            pltpu.VMEM((2 * tm * SUBLANES, LANES), F32),
                            pltpu.SMEM((3 * 2 * IDX_WIN,), jnp.int32),
                            pltpu.SemaphoreType.DMA((2,)), pltpu.SemaphoreType.DMA((2,)),
                            pltpu.SemaphoreType.DMA((3,))]),
        out_shape=jax.ShapeDtypeStruct((TOP_K * x1s.shape[0], LANES), F32),
        compiler_params=_params("arbitrary"),
        name="moe_experts",
    )(block_e, n_used, blk_src, blk_cnt, gsrc, sdst, x1s, w_gate_up, bg, bl, w_down, bd)


def _combine_kernel(gate_ref, x1_ref, o4_ref, p_ref, wpg_ref, wpp_ref, g_ref, b_ref, o_ref):
    tm, d = o_ref.shape
    pp = jnp.dot(p_ref[...].astype(BF16), wpp_ref[...], preferred_element_type=F32)
    gates = gate_ref[...]
    moe = gates[:, 0:1] * _slab_load(o4_ref, tm, d, lead=(0,))
    for k in range(1, TOP_K):
        moe = moe + gates[:, k:k + 1] * _slab_load(o4_ref, tm, d, lead=(k,))
    r = DEEPNORM_ALPHA * _slab_load(x1_ref, tm, d) + moe
    gate = jax.nn.sigmoid(jnp.dot(r.astype(BF16), wpg_ref[...], preferred_element_type=F32))
    o_ref[...] = _layer_norm(r + gate * pp, g_ref[...], b_ref[...])


def _combine(gates_t, x1s, out4, p2, w_ple_gate, w_ple_proj, ln_g, ln_b):
    T, pd = p2.shape
    D = w_ple_gate.shape[0]
    tm = COMBINE_ROWS
    full = lambda shape: pl.BlockSpec(shape, lambda i: (0,) * len(shape))
    return pl.pallas_call(
        _combine_kernel,
        grid=(T // tm,),
        in_specs=[pl.BlockSpec((tm, TOP_K), lambda i: (i, 0)),
                  pl.BlockSpec((tm * SUBLANES, LANES), lambda i: (i, 0)),
                  pl.BlockSpec((TOP_K, tm * SUBLANES, LANES), lambda i: (0, i, 0)),
                  pl.BlockSpec((tm, pd), lambda i: (i, 0)),
                  full((D, D)), full((pd, D)), full((1, D)), full((1, D))],
        out_specs=pl.BlockSpec((tm, D), lambda i: (i, 0)),
        out_shape=jax.ShapeDtypeStruct((T, D), F32),
        compiler_params=_params("parallel"),
        name="moe_combine",
    )(gates_t, x1s, out4, p2, w_ple_gate.astype(BF16), w_ple_proj.astype(BF16),
      ln_g.reshape(1, D).astype(F32), ln_b.reshape(1, D).astype(F32))


def kernel(x, p, w_in, lambda_q1, lambda_k1, lambda_q2, lambda_k2, subln_g, ssm_a_re, ssm_a_im,
           ssm_log_dt, ssm_b_re, ssm_b_im, ssm_c_re, ssm_c_im, ssm_d, w_glu, ssm_norm_g, w_out,
           ln1_g, ln1_b, w_router, b_router, w_gate_up, b_gate_up, w_down, b_down, w_ple_gate,
           w_ple_proj, ln2_g, ln2_b):
    B, L, D = x.shape
    T = B * L
    assert D == SUBLANES * LANES, "token-slab layout holds one token per (8, 128) tile"
    for i in range(DEPTH):
        lambda_init = 0.8 - 0.6 * math.exp(-0.3 * i)
        q, kt, v, u = _in_proj(x, w_in[i])
        lam_vecs = jnp.stack([lambda_q1[i], lambda_k1[i], lambda_q2[i], lambda_k2[i]]).astype(F32)
        attn = _attention(q, kt, v, lam_vecs, subln_g[i], lambda_init)
        b_fold, c_fold, a_tile = _ssm_fold_params(ssm_a_re[i], ssm_a_im[i], ssm_log_dt[i],
                                                  ssm_b_re[i], ssm_b_im[i], ssm_c_re[i],
                                                  ssm_c_im[i], B)
        ssm = _ssm(u, b_fold, c_fold, a_tile, ssm_d[i], w_glu[i], ssm_norm_g[i])

        x1, idx, gates, counts = _mix_route(
            x.reshape(T, D), attn.reshape(T, -1), ssm.reshape(T, -1), w_out[i], ln1_g[i],
            ln1_b[i], w_router[i], b_router[i])

        order = jnp.argsort(idx.T.reshape(-1), stable=True).astype(jnp.int32)
        order = jnp.concatenate([order, jnp.zeros((IDX_WIN,), jnp.int32)])
        tok_row = lax.shift_right_logical(order, TOP_K_SHIFT) * SUBLANES
        gsrc = tok_row
        sdst = (order & (TOP_K - 1)) * (T * SUBLANES) + tok_row
        cnt = counts[:, 0].astype(jnp.int32)
        start = jnp.cumsum(cnt) - cnt
        n_blk_e = (cnt + MOE_ROWS - 1) // MOE_ROWS
        blk_end = jnp.cumsum(n_blk_e)
        n_blocks = (T * TOP_K) // MOE_ROWS + N_EXPERTS
        blk = jnp.arange(n_blocks, dtype=jnp.int32)
        block_e = jnp.minimum(jnp.sum(blk_end[None, :] <= blk[:, None], axis=1),
                              N_EXPERTS - 1).astype(jnp.int32)
        local = blk - (blk_end - n_blk_e)[block_e]
        blk_src = (start[block_e] + local * MOE_ROWS).astype(jnp.int32)
        blk_cnt = jnp.clip(cnt[block_e] - local * MOE_ROWS, 0, MOE_ROWS).astype(jnp.int32)
        n_used = blk_end[-1:].astype(jnp.int32)
        blk_src = jnp.where(blk < n_used, blk_src, 0)
        blk_cnt = jnp.where(blk < n_used, blk_cnt, 0)

        out4 = _experts(x1, gsrc, sdst, block_e, n_used, blk_src, blk_cnt, w_gate_up[i],
                        b_gate_up[i], w_down[i], b_down[i])
        x = _combine(gates.T, x1, out4.reshape(TOP_K, T * SUBLANES, LANES), p[i].reshape(T, -1), w_ple_gate[i], w_ple_proj[i],
                     ln2_g[i], ln2_b[i]).reshape(B, L, D)
    return x
```

```python
import functools
import math

import jax
import jax.numpy as jnp
from jax import lax
from jax.experimental import pallas as pl
from jax.experimental.pallas import tpu as pltpu

F32 = jnp.float32
BF16 = jnp.bfloat16

N_HEADS = 8
HEAD_DIM = 32
V_DIM = 2 * HEAD_DIM
D_ATTN = N_HEADS * V_DIM
SSM_GROUP = 16
SSM_STATE = 64
N_EXPERTS = 32
TOP_K = 4
SWIGLU_LIMIT = 7.0
SWIGLU_ALPHA = 1.702
LN_EPS = 1e-5
RMS_EPS = 1e-5
DEPTH = 1
DEEPNORM_ALPHA = (2.0 * DEPTH) ** 0.25
LOG2E = math.log2(math.e)

LANES = 128
SUBLANES = 8
VMEM_LIMIT = 56 * 1024 * 1024

PROJ_ROWS = 512
ATTN_Q = 512
BIAS_PARTS = 3
SSM_CHUNK = 128
SSM_MM_ROWS = 256
MIX_ROWS = 512
MOE_ROWS = 256
SPLIT_COLS = 512
WPREP_ROWS = 256
COMBINE_ROWS = 256


def _params(*sem):
    return pltpu.CompilerParams(dimension_semantics=sem, vmem_limit_bytes=VMEM_LIMIT)


def _in_proj_kernel(x_ref, wq_ref, wkt_ref, wv_ref, wu_ref, q_ref, kt_ref, v_ref, u_ref):
    xb = x_ref[0].astype(BF16)
    q = jnp.dot(xb, wq_ref[...], preferred_element_type=F32)
    q_ref[0] = (q * (HEAD_DIM ** -0.5 * LOG2E)).astype(BF16)
    kt = lax.dot_general(wkt_ref[...], xb, (((1,), (1,)), ((), ())),
                         preferred_element_type=F32)
    kt_ref[0, 0] = kt.astype(BF16)
    lane_v = lax.broadcasted_iota(jnp.int32, (1, wv_ref.shape[1]), 1)
    ones_lane = (lane_v % LANES == V_DIM).astype(F32)
    v_ref[0] = (jnp.dot(xb, wv_ref[...], preferred_element_type=F32) + ones_lane).astype(BF16)
    u_ref[0] = jnp.dot(xb, wu_ref[...], preferred_element_type=F32)


def _in_proj(x, w_in):
    B, L, D = x.shape
    d_ssm = w_in.shape[1] - 3 * D_ATTN
    tm = PROJ_ROWS
    nt = L // tm
    wb = w_in.astype(BF16)
    wq = wb[:, :D_ATTN]
    wkt = wb[:, D_ATTN:2 * D_ATTN].T
    wv = wb[:, 2 * D_ATTN:3 * D_ATTN].reshape(D, N_HEADS, V_DIM)
    wv = jnp.pad(wv, ((0, 0), (0, 0), (0, LANES - V_DIM))).reshape(D, N_HEADS * LANES)
    wu = wb[:, 3 * D_ATTN:]
    full = lambda shape: pl.BlockSpec(shape, lambda b, i: (0,) * len(shape))
    return pl.pallas_call(
        _in_proj_kernel,
        grid=(B, nt),
        in_specs=[pl.BlockSpec((1, tm, D), lambda b, i: (b, i, 0)),
                  full((D, D_ATTN)), full((D_ATTN, D)), full((D, N_HEADS * LANES)),
                  full((D, d_ssm))],
        out_specs=[pl.BlockSpec((1, tm, D_ATTN), lambda b, i: (b, i, 0)),
                   pl.BlockSpec((1, 1, D_ATTN, tm), lambda b, i: (b, i, 0, 0)),
                   pl.BlockSpec((1, tm, N_HEADS * LANES), lambda b, i: (b, i, 0)),
                   pl.BlockSpec((1, tm, d_ssm), lambda b, i: (b, i, 0))],
        out_shape=[jax.ShapeDtypeStruct((B, L, D_ATTN), BF16),
                   jax.ShapeDtypeStruct((B, nt, D_ATTN, tm), BF16),
                   jax.ShapeDtypeStruct((B, L, N_HEADS * LANES), BF16),
                   jax.ShapeDtypeStruct((B, L, d_ssm), F32)],
        compiler_params=_params("parallel", "parallel"),
        name="in_proj",
    )(x, wq, wkt, wv, wu)


def _attn_kernel(slope_ref, lam_ref, q_ref, kt_ref, v_ref, g_ref, o_ref,
                 q4_ref, jr_ref, p_ref, m_ref, acc_ref, *, tq, tk, lambda_init):
    hp = pl.program_id(1)
    qi = pl.program_id(2)
    q0 = qi * tq
    n_cb = tk // LANES

    q = q_ref[0]
    lane_q = lax.broadcasted_iota(jnp.int32, q.shape, 1)
    for c in range(4):
        rs = slice(c * tq, (c + 1) * tq)
        lo = (c // 2) * BIAS_PARTS
        q4_ref[rs, :LANES] = jnp.where(lane_q // HEAD_DIM == c, q, jnp.zeros_like(q))
        q4_ref[rs, LANES:] = ((lane_q >= lo) & (lane_q < lo + BIAS_PARTS)).astype(BF16)

    @pl.when(qi == 0)
    def _():
        colf = lax.broadcasted_iota(jnp.int32, (1, tk), 1).astype(F32)
        r_i = lax.broadcasted_iota(jnp.int32, (LANES, tk), 0)
        jr = jnp.zeros((LANES, tk), F32)
        for h in range(2):
            rem = (slope_ref[2 * hp + h] * LOG2E) * colf
            for part in range(BIAS_PARTS):
                piece = rem.astype(BF16).astype(F32)
                jr = jnp.where(r_i == h * BIAS_PARTS + part, piece, jr)
                rem = rem - piece
        jr_ref[...] = jr.astype(BF16)

    m_ref[...] = jnp.full(m_ref.shape, -jnp.inf, F32)
    acc_ref[...] = jnp.zeros(acc_ref.shape, F32)

    col = lax.broadcasted_iota(jnp.int32, (1, tk), 1)
    row = lax.broadcasted_iota(jnp.int32, (tq, 1), 0)
    zero_row = jnp.zeros((1, LANES), jnp.int32)

    def v_tile(t):
        return v_ref[0, pl.ds(pl.multiple_of(t * tk, tk), tk), :]

    def tile(t, masked):
        kt_aug = jnp.concatenate([kt_ref[0, t], jr_ref[...]], axis=0)
        s_all = jnp.dot(q4_ref[...], kt_aug, preferred_element_type=F32)
        rel = t * tk - q0 + col
        base = (t * tk - q0 + zero_row).astype(F32)
        for c in range(4):
            rs = slice(c * tq, (c + 1) * tq)
            off = (slope_ref[2 * hp + c // 2] * LOG2E) * base
            s = s_all[rs]
            if masked:
                s = jnp.where(rel <= row, s, -jnp.inf)
            blocks = [s[:, cb * LANES:(cb + 1) * LANES] for cb in range(n_cb)]
            part = blocks[0]
            for blk in blocks[1:]:
                part = jnp.maximum(part, blk)
            m_old = m_ref[rs]
            m_new = jnp.maximum(m_old, jnp.max(part, axis=1, keepdims=True) + off)
            m_ref[rs] = m_new
            alpha = jnp.exp2(m_old - m_new)
            shift = m_new - off
            acc_ref[rs] = alpha * acc_ref[rs]
            p_ref[rs] = jnp.concatenate([jnp.exp2((blk - shift).astype(BF16)) for blk in blocks],
                                        axis=1)
        v2 = v_tile(t)
        for j in range(2):
            hr = slice(2 * j * tq, (2 * j + 2) * tq)
            acc_ref[hr] += jnp.dot(p_ref[hr], v2[:, j * LANES:(j + 1) * LANES],
                                   preferred_element_type=F32)

    n_full = q0 // tk

    def body(t, carry):
        tile(t, False)
        return carry

    lax.fori_loop(0, n_full, body, 0)
    tile(n_full, True)
    acc = acc_ref[...]

    lam = lam_ref[...]
    lane = lax.broadcasted_iota(jnp.int32, (1, LANES), 1)
    outs = []
    for j in range(2):
        o = []
        for c in (2 * j, 2 * j + 1):
            a = acc[c * tq:(c + 1) * tq]
            denom = jnp.sum(jnp.where(lane == V_DIM, a, 0.0), axis=1, keepdims=True)
            o.append(a / denom)
        d = o[0] - lam * o[1]
        ms = jnp.sum(jnp.where(lane < V_DIM, d * d, 0.0), axis=1, keepdims=True) * (1.0 / V_DIM)
        outs.append(d * lax.rsqrt(ms + RMS_EPS))
    out = jnp.where(lane < V_DIM, outs[0], pltpu.roll(outs[1], V_DIM, 1))
    out = out * g_ref[...] * (1.0 - lambda_init)
    o_ref[0] = out.astype(o_ref.dtype)


def _attention(q, kt, v, lam_vecs, subln_g, lambda_init):
    B, L, _ = q.shape
    tq, tk = ATTN_Q, PROJ_ROWS
    nk = L // tk
    slopes = jnp.exp2(-(jnp.arange(N_HEADS, dtype=F32) + 1.0) * (8.0 / N_HEADS))
    lam = (jnp.exp(jnp.sum(lam_vecs[0] * lam_vecs[1])) - jnp.exp(jnp.sum(lam_vecs[2] * lam_vecs[3]))
           + lambda_init).reshape(1, 1).astype(F32)
    g2 = jnp.tile(subln_g.astype(F32), 2).reshape(1, LANES)
    kern = functools.partial(_attn_kernel, tq=tq, tk=tk, lambda_init=lambda_init)
    return pl.pallas_call(
        kern,
        grid_spec=pltpu.PrefetchScalarGridSpec(
            num_scalar_prefetch=1,
            grid=(B, N_HEADS // 2, L // tq),
            in_specs=[pl.BlockSpec((1, 1), lambda b, h, i, s: (0, 0)),
                      pl.BlockSpec((1, tq, LANES), lambda b, h, i, s: (b, i, h)),
                      pl.BlockSpec((1, nk, LANES, tk), lambda b, h, i, s: (b, 0, h, 0)),
                      pl.BlockSpec((1, L, 2 * LANES), lambda b, h, i, s: (b, 0, h)),
                      pl.BlockSpec((1, LANES), lambda b, h, i, s: (0, 0))],
            out_specs=pl.BlockSpec((1, tq, LANES), lambda b, h, i, s: (b, i, h)),
            scratch_shapes=[pltpu.VMEM((4 * tq, 2 * LANES), BF16),
                            pltpu.VMEM((LANES, tk), BF16),
                            pltpu.VMEM((4 * tq, tk), BF16),
                            pltpu.VMEM((4 * tq, LANES), F32),
                            pltpu.VMEM((4 * tq, LANES), F32)]),
        out_shape=jax.ShapeDtypeStruct((B, L, D_ATTN), BF16),
        compiler_params=_params("parallel", "parallel", "arbitrary"),
        name="diff_attn",
    )(slopes, lam, q, kt, v, g2)


def _ssm_fold_params(a_re, a_im, log_dt, b_re, b_im, c_re, c_im, batch):
    G, P = a_re.shape
    C = b_re.shape[-1]
    half = G // 2
    A = lax.complex(a_re.astype(F32), a_im.astype(F32))
    dt = jnp.exp(log_dt.astype(F32))[:, None]
    a_bar = jnp.exp(A * dt)
    b_bar = ((a_bar - 1.0) / A)[..., None] * lax.complex(b_re.astype(F32), b_im.astype(F32))
    sel = jnp.eye(half, dtype=F32)[jnp.arange(G) % half]
    fold_b = lambda m: jnp.einsum('gpc,gk->gckp', m, sel).reshape(G * C, half * P)
    b_fold = jnp.concatenate([fold_b(jnp.real(b_bar)), fold_b(jnp.imag(b_bar))], axis=1)
    fold_c = lambda m: jnp.einsum('gcp,gk->kpgc', m, sel).reshape(half * P, G * C)
    c_fold = jnp.concatenate([fold_c(c_re.astype(F32)), -fold_c(c_im.astype(F32))], axis=0)
    a_rows = jnp.tile(a_bar.reshape(2, half * P), (batch, 1))
    a_tile = jnp.concatenate([jnp.real(a_rows), jnp.imag(a_rows)], axis=1)
    return b_fold.astype(BF16), c_fold.astype(BF16), a_tile.astype(F32)


def _gelu_tanh(x):
    return 0.5 * x * (1.0 + jnp.tanh(math.sqrt(2.0 / math.pi) * (x + 0.044715 * (x * x * x))))


def _ssm_kernel(u_ref, bf_ref, cf_ref, a_ref, d_ref, wglu_ref, gn_ref, o_ref,
                lhs_ref, x_ref, y_ref, s_ref, *, chunk, batch):
    i = pl.program_id(0)
    d_ssm = u_ref.shape[-1]
    n_slab = d_ssm // LANES
    ns = x_ref.shape[-1] // 2
    rows = SUBLANES

    @pl.when(i == 0)
    def _():
        s_ref[...] = jnp.zeros_like(s_ref)

    zero = jnp.zeros((chunk, LANES), F32)
    for b in range(batch):
        ub = u_ref[b]
        for h in range(2):
            for c in range(n_slab):
                src = ub[:, c * LANES:(c + 1) * LANES] if (c * 2) // n_slab == h else zero
                lhs_ref[c, pl.ds(2 * b + h, chunk, stride=rows), :] = src
    n_blk = (rows * chunk) // SSM_MM_ROWS

    hs = ns // 2
    n_set = 2

    def in_mm(rb, carry):
        rs = pl.ds(pl.multiple_of(rb * SSM_MM_ROWS, SSM_MM_ROWS), SSM_MM_ROWS)
        for s in range(n_set):
            lhs = jnp.concatenate([lhs_ref[s, rs, :], lhs_ref[s + n_slab // 2, rs, :]],
                                  axis=1).astype(BF16)
            xs = jnp.dot(lhs, bf_ref[s], preferred_element_type=F32)
            x_ref[rs, s * hs:(s + 1) * hs] = xs[:, :hs]
            x_ref[rs, ns + s * hs:ns + (s + 1) * hs] = xs[:, hs:]
        return carry

    lax.fori_loop(0, n_blk, in_mm, 0)

    a_re = a_ref[:, :ns]
    a_im = a_ref[:, ns:]

    def step(t, carry):
        s_re, s_im = carry
        r0 = pl.multiple_of(t * rows, rows)
        n_re = a_re * s_re - a_im * s_im + x_ref[pl.ds(r0, rows), :ns]
        n_im = a_re * s_im + a_im * s_re + x_ref[pl.ds(r0, rows), ns:]
        x_ref[pl.ds(r0, rows), :ns] = n_re
        x_ref[pl.ds(r0, rows), ns:] = n_im
        return n_re, n_im

    s_re, s_im = lax.fori_loop(0, chunk, step, (s_ref[:, :ns], s_ref[:, ns:]), unroll=2)
    s_ref[:, :ns] = s_re
    s_ref[:, ns:] = s_im

    def out_mm(rb, carry):
        rs = pl.ds(pl.multiple_of(rb * SSM_MM_ROWS, SSM_MM_ROWS), SSM_MM_ROWS)
        for s in range(n_set):
            st = jnp.concatenate([x_ref[rs, s * hs:(s + 1) * hs],
                                  x_ref[rs, ns + s * hs:ns + (s + 1) * hs]], axis=1).astype(BF16)
            y = jnp.dot(st, cf_ref[s], preferred_element_type=F32)
            y_ref[s, rs, :] = y[:, :LANES]
            y_ref[s + n_slab // 2, rs, :] = y[:, LANES:]
        return carry

    lax.fori_loop(0, n_blk, out_mm, 0)

    for b in range(batch):
        parts = []
        for c in range(n_slab):
            h = (c * 2) // n_slab
            parts.append(y_ref[c, pl.ds(2 * b + h, chunk, stride=rows), :])
        yb = jnp.concatenate(parts, axis=1) + d_ref[...] * u_ref[b]
        g = jnp.dot(_gelu_tanh(yb).astype(BF16), wglu_ref[...], preferred_element_type=F32)
        z = g[:, :d_ssm] * jax.nn.sigmoid(g[:, d_ssm:])
        ms = jnp.mean(z * z, axis=1, keepdims=True)
        o_ref[b] = (z * lax.rsqrt(ms + RMS_EPS) * gn_ref[...]).astype(o_ref.dtype)


def _ssm(u, b_fold, c_fold, a_tile, d_skip, w_glu, norm_g):
    B, L, d_ssm = u.shape
    assert 2 * B == SUBLANES, "row-stream layout packs batch x 2 group halves on 8 sublanes"
    chunk = SSM_CHUNK
    ns2 = a_tile.shape[1]
    n_slab = d_ssm // LANES
    assert n_slab == 4, "two group sets x two group halves of 128 channels"
    ns, hs = ns2 // 2, ns2 // 4
    ch = lambda s: jnp.r_[s * LANES:(s + 1) * LANES, (s + 2) * LANES:(s + 3) * LANES]
    st = lambda s: jnp.r_[s * hs:(s + 1) * hs, ns + s * hs:ns + (s + 1) * hs]
    b_sets = jnp.stack([b_fold[ch(s)][:, st(s)] for s in range(2)])
    c_sets = jnp.stack([c_fold[st(s)][:, ch(s)] for s in range(2)])
    full = lambda shape: pl.BlockSpec(shape, lambda i: (0,) * len(shape))
    kern = functools.partial(_ssm_kernel, chunk=chunk, batch=B)
    return pl.pallas_call(
        kern,
        grid=(L // chunk,),
        in_specs=[pl.BlockSpec((B, chunk, d_ssm), lambda i: (0, i, 0)),
                  full(b_sets.shape), full(c_sets.shape), full((SUBLANES, ns2)),
                  full((1, d_ssm)), full((d_ssm, 2 * d_ssm)), full((1, d_ssm))],
        out_specs=pl.BlockSpec((B, chunk, d_ssm), lambda i: (0, i, 0)),
        out_shape=jax.ShapeDtypeStruct((B, L, d_ssm), BF16),
        scratch_shapes=[pltpu.VMEM((n_slab, SUBLANES * chunk, LANES), F32),
                        pltpu.VMEM((SUBLANES * chunk, ns2), F32),
                        pltpu.VMEM((n_slab, SUBLANES * chunk, LANES), F32),
                        pltpu.VMEM((SUBLANES, ns2), F32)],
        compiler_params=_params("arbitrary"),
        name="s5_ssm",
    )(u, b_sets, c_sets, a_tile, d_skip.reshape(1, d_ssm).astype(F32), w_glu.astype(BF16),
      norm_g.reshape(1, d_ssm).astype(F32))


def _layer_norm(x, g, b):
    mu = jnp.mean(x, axis=1, keepdims=True)
    xc = x - mu
    var = jnp.mean(xc * xc, axis=1, keepdims=True)
    return xc * lax.rsqrt(var + LN_EPS) * g + b


def _slab_load(ref, n_rows, d, lead=()):
    return jnp.concatenate(
        [ref[lead + (pl.ds(c, n_rows, stride=SUBLANES), slice(None))] for c in range(d // LANES)],
        axis=1)


def _slab_store(ref, val):
    n_rows, d = val.shape
    for c in range(d // LANES):
        ref[pl.ds(c, n_rows, stride=SUBLANES), :] = val[:, c * LANES:(c + 1) * LANES]


def _mix_kernel(x_ref, attn_ref, ssm_ref, woa_ref, wos_ref, g_ref, b_ref, wrt_ref, br_ref,
                x1_ref, idx_ref, gate_ref, cnt_ref, carry_ref):
    i = pl.program_id(0)
    tm = x_ref.shape[0]

    @pl.when(i == 0)
    def _():
        carry_ref[...] = jnp.zeros_like(carry_ref)

    mix = (jnp.dot(attn_ref[...], woa_ref[...], preferred_element_type=F32)
           + jnp.dot(ssm_ref[...], wos_ref[...], preferred_element_type=F32))
    x1 = _layer_norm(DEEPNORM_ALPHA * x_ref[...] + mix, g_ref[...], b_ref[...])
    _slab_store(x1_ref, x1)

    logits = lax.dot_general(wrt_ref[...], x1.astype(BF16), (((1,), (1,)), ((), ())),
                             preferred_element_type=F32) + br_ref[...]
    e_iota = lax.broadcasted_iota(jnp.int32, logits.shape, 0).astype(F32)
    work = logits
    vals, hots = [], []
    for k in range(TOP_K):
        mx = jnp.max(work, axis=0, keepdims=True)
        idx = jnp.min(jnp.where(work == mx, e_iota, float(N_EXPERTS)), axis=0, keepdims=True)
        hot = e_iota == idx
        work = jnp.where(hot, -jnp.inf, work)
        vals.append(mx)
        hots.append(hot)
        idx_ref[k:k + 1, :] = idx.astype(jnp.int32)
    ex = [jnp.exp(v - vals[0]) for v in vals]
    den = ex[0] + ex[1] + ex[2] + ex[3]
    for k in range(TOP_K):
        gate_ref[k:k + 1, :] = ex[k] / den

    sel = (hots[0] | hots[1] | hots[2] | hots[3]).astype(F32)
    carry_ref[...] = carry_ref[...] + jnp.sum(sel, axis=1, keepdims=True)
    cnt_ref[...] = carry_ref[...]


def _mix_route(x2, attn2, ssm2, w_out, ln_g, ln_b, w_router, b_router):
    T, D = x2.shape
    tm = MIX_ROWS
    d_attn = attn2.shape[1]
    wo = w_out.astype(BF16)
    full = lambda shape: pl.BlockSpec(shape, lambda i: (0,) * len(shape))
    rowblk = lambda w: pl.BlockSpec((tm, w), lambda i: (i, 0))
    colblk = pl.BlockSpec((TOP_K, tm), lambda i: (0, i))
    return pl.pallas_call(
        _mix_kernel,
        grid=(T // tm,),
        in_specs=[rowblk(D), rowblk(d_attn), rowblk(ssm2.shape[1]),
                  full((d_attn, D)), full((D - d_attn, D)), full((1, D)), full((1, D)),
                  full((N_EXPERTS, D)), full((N_EXPERTS, 1))],
        out_specs=[pl.BlockSpec((tm * SUBLANES, LANES), lambda i: (i, 0)),
                   colblk, colblk, full((N_EXPERTS, LANES))],
        out_shape=[jax.ShapeDtypeStruct((T * SUBLANES, LANES), F32),
                   jax.ShapeDtypeStruct((TOP_K, T), jnp.int32),
                   jax.ShapeDtypeStruct((TOP_K, T), F32),
                   jax.ShapeDtypeStruct((N_EXPERTS, LANES), F32)],
        scratch_shapes=[pltpu.VMEM((N_EXPERTS, LANES), F32)],
        compiler_params=_params("arbitrary"),
        name="mix_route",
    )(x2, attn2, ssm2, wo[:d_attn], wo[d_attn:], ln_g.reshape(1, D).astype(F32),
      ln_b.reshape(1, D).astype(F32), w_router.T.astype(BF16),
      b_router.reshape(N_EXPERTS, 1).astype(F32))


def _expert_kernel(be_ref, nu_ref, cnt_ref, gs0_ref, gs1_ref, sd_ref, x1_ref, wgu_ref, bg_ref,
                   bl_ref, wd_ref, bd_ref, o4_ref, sel_ref, wgl_ref, wdb_ref, xbuf, obuf,
                   gsem, ssem):
    i = pl.program_id(0)
    nu = nu_ref[0]
    _, d, f2 = wgu_ref.shape
    f = f2 // 2
    half = SPLIT_COLS // 2
    active = i < nu
    blk_rows = MOE_ROWS * SUBLANES

    def rows(start, n=SUBLANES):
        if isinstance(start, int):
            return pl.ds(start, n)
        return pl.ds(pl.multiple_of(start, SUBLANES), n)

    def gather_row(slot, gs_ref, r):
        return pltpu.make_async_copy(x1_ref.at[rows(gs_ref[r])],
                                     xbuf.at[rows(slot * blk_rows + r * SUBLANES)],
                                     gsem.at[slot])

    def scatter_row(slot, r):
        return pltpu.make_async_copy(obuf.at[rows(slot * blk_rows + r * SUBLANES)],
                                     o4_ref.at[rows(sd_ref[r])], ssem.at[slot])

    def start_block(j, row_copy):
        for slot in range(2):
            @pl.when((j & 1) == slot)
            def _():
                for r in range(MOE_ROWS):
                    row_copy(slot, r).start(priority=r % 2)

    def wait_gather(j):
        pltpu.make_async_copy(x1_ref.at[rows(0, blk_rows)],
                              xbuf.at[rows((j & 1) * blk_rows, blk_rows)], gsem.at[j & 1]).wait()

    def start_scatter(j):
        @pl.when(cnt_ref[j] == MOE_ROWS)
        def _():
            start_block(j, scatter_row)

        @pl.when(cnt_ref[j] < MOE_ROWS)
        def _():
            def one(r, c):
                scatter_row(j & 1, r).start()
                return c

            lax.fori_loop(0, cnt_ref[j], one, 0)

    def wait_scatter(j):
        @pl.when(cnt_ref[j] == MOE_ROWS)
        def _():
            pltpu.make_async_copy(obuf.at[rows((j & 1) * blk_rows, blk_rows)],
                                  o4_ref.at[rows(0, blk_rows)], ssem.at[j & 1]).wait()

        @pl.when(cnt_ref[j] < MOE_ROWS)
        def _():
            def one(r, c):
                pltpu.make_async_copy(obuf.at[rows(0)], o4_ref.at[rows(0)], ssem.at[j & 1]).wait()
                return c

            lax.fori_loop(0, cnt_ref[j], one, 0)

    @pl.when(i == 0)
    def _():
        r = lax.broadcasted_iota(jnp.int32, sel_ref.shape, 0)
        c = lax.broadcasted_iota(jnp.int32, sel_ref.shape, 1)
        src = jnp.where(c < half, 2 * c, 2 * (c - half) + 1)
        sel_ref[...] = (r == src).astype(BF16)

        def first(r, c):
            gather_row(0, gs0_ref, r).start()
            return c

        lax.fori_loop(0, MOE_ROWS, first, 0)

    @pl.when(active)
    def _():
        wait_gather(i)

        @pl.when(i + 1 < nu)
        def _():
            start_block(i + 1, lambda slot, r: gather_row(slot, gs1_ref, r))

    new_expert = (i == 0) | (be_ref[i] != be_ref[jnp.maximum(i - 1, 0)])

    @pl.when(active & new_expert)
    def _():
        def split_rows(rb, carry):
            rs = pl.ds(pl.multiple_of(rb * WPREP_ROWS, WPREP_ROWS), WPREP_ROWS)
            for j in range(f2 // SPLIT_COLS):
                w = wgu_ref[0, rs, j * SPLIT_COLS:(j + 1) * SPLIT_COLS].astype(BF16)
                y = jnp.dot(w, sel_ref[...], preferred_element_type=F32).astype(BF16)
                wgl_ref[rs, j * half:(j + 1) * half] = y[:, :half]
                wgl_ref[rs, f + j * half:f + (j + 1) * half] = y[:, half:]
            return carry

        def cast_rows(rb, carry):
            rs = pl.ds(pl.multiple_of(rb * WPREP_ROWS, WPREP_ROWS), WPREP_ROWS)
            wdb_ref[rs, :] = wd_ref[0, rs, :].astype(BF16)
            return carry

        lax.fori_loop(0, d // WPREP_ROWS, split_rows, 0)
        lax.fori_loop(0, f // WPREP_ROWS, cast_rows, 0)

    @pl.when(active)
    def _():
        base = pl.multiple_of((i & 1) * blk_rows, blk_rows)
        xb = _slab_load(xbuf.at[pl.ds(base, blk_rows)], MOE_ROWS, d).astype(BF16)
        h = jnp.dot(xb, wgl_ref[...], preferred_element_type=F32)
        xg = jnp.minimum(h[:, :f] + bg_ref[0], SWIGLU_LIMIT)
        xl = jnp.clip(h[:, f:] + bl_ref[0], -SWIGLU_LIMIT, SWIGLU_LIMIT)
        act = xg * jax.nn.sigmoid(SWIGLU_ALPHA * xg) * (xl + 1.0)
        out = jnp.dot(act.astype(BF16), wdb_ref[...], preferred_element_type=F32) + bd_ref[0]
        _slab_store(obuf.at[pl.ds(base, blk_rows)], out)

        start_scatter(i)

        @pl.when(i >= 1)
        def _():
            wait_scatter(i - 1)

        @pl.when(i == nu - 1)
        def _():
            wait_scatter(i)


def _experts(x1s, gsrc, sdst, block_e, n_used, blk_cnt, w_gate_up, b_gate_up, w_down, b_down):
    E, D, F2 = w_gate_up.shape
    F = F2 // 2
    tm = MOE_ROWS
    n_blocks = block_e.shape[0]
    bg = b_gate_up[:, 0::2].reshape(E, 1, F).astype(F32)
    bl = b_gate_up[:, 1::2].reshape(E, 1, F).astype(F32)
    bd = b_down.reshape(E, 1, D).astype(F32)
    wmap = lambda i, be, nu, cn: (be[i], 0, 0)
    any_spec = pl.BlockSpec(memory_space=pl.ANY)
    slots = lambda ahead: pl.BlockSpec(
        (tm,), lambda i, be, nu, cn: (jnp.minimum(i + ahead, nu[0] - 1),), memory_space=pltpu.SMEM)
    return pl.pallas_call(
        _expert_kernel,
        grid_spec=pltpu.PrefetchScalarGridSpec(
            num_scalar_prefetch=3,
            grid=(n_blocks,),
            in_specs=[slots(0), slots(1), slots(0), any_spec,
                      pl.BlockSpec((1, D, F2), wmap),
                      pl.BlockSpec((1, 1, F), wmap), pl.BlockSpec((1, 1, F), wmap),
                      pl.BlockSpec((1, F, D), wmap), pl.BlockSpec((1, 1, D), wmap)],
            out_specs=any_spec,
            scratch_shapes=[pltpu.VMEM((SPLIT_COLS, SPLIT_COLS), BF16),
                            pltpu.VMEM((D, F2), BF16),
                            pltpu.VMEM((F, D), BF16),
                            pltpu.VMEM((2 * tm * SUBLANES, LANES), F32),
                            pltpu.VMEM((2 * tm * SUBLANES, LANES), F32),
                            pltpu.SemaphoreType.DMA((2,)), pltpu.SemaphoreType.DMA((2,))]),
        out_shape=jax.ShapeDtypeStruct((TOP_K * x1s.shape[0], LANES), F32),
        compiler_params=_params("arbitrary"),
        name="moe_experts",
    )(block_e, n_used, blk_cnt, gsrc, gsrc, sdst, x1s, w_gate_up, bg, bl, w_down, bd)


def _combine_kernel(gate_ref, x1_ref, o4_ref, p_ref, wpg_ref, wpp_ref, g_ref, b_ref, o_ref):
    tm, d = o_ref.shape
    pp = jnp.dot(p_ref[...].astype(BF16), wpp_ref[...], preferred_element_type=F32)
    gates = gate_ref[...]
    moe = gates[:, 0:1] * _slab_load(o4_ref, tm, d, lead=(0,))
    for k in range(1, TOP_K):
        moe = moe + gates[:, k:k + 1] * _slab_load(o4_ref, tm, d, lead=(k,))
    r = DEEPNORM_ALPHA * _slab_load(x1_ref, tm, d) + moe
    gate = jax.nn.sigmoid(jnp.dot(r.astype(BF16), wpg_ref[...], preferred_element_type=F32))
    o_ref[...] = _layer_norm(r + gate * pp, g_ref[...], b_ref[...])


def _combine(gates_t, x1s, out4, p2, w_ple_gate, w_ple_proj, ln_g, ln_b):
    T, pd = p2.shape
    D = w_ple_gate.shape[0]
    tm = COMBINE_ROWS
    full = lambda shape: pl.BlockSpec(shape, lambda i: (0,) * len(shape))
    return pl.pallas_call(
        _combine_kernel,
        grid=(T // tm,),
        in_specs=[pl.BlockSpec((tm, TOP_K), lambda i: (i, 0)),
                  pl.BlockSpec((tm * SUBLANES, LANES), lambda i: (i, 0)),
                  pl.BlockSpec((TOP_K, tm * SUBLANES, LANES), lambda i: (0, i, 0)),
                  pl.BlockSpec((tm, pd), lambda i: (i, 0)),
                  full((D, D)), full((pd, D)), full((1, D)), full((1, D))],
        out_specs=pl.BlockSpec((tm, D), lambda i: (i, 0)),
        out_shape=jax.ShapeDtypeStruct((T, D), F32),
        compiler_params=_params("parallel"),
        name="moe_combine",
    )(gates_t, x1s, out4, p2, w_ple_gate.astype(BF16), w_ple_proj.astype(BF16),
      ln_g.reshape(1, D).astype(F32), ln_b.reshape(1, D).astype(F32))


def kernel(x, p, w_in, lambda_q1, lambda_k1, lambda_q2, lambda_k2, subln_g, ssm_a_re, ssm_a_im,
           ssm_log_dt, ssm_b_re, ssm_b_im, ssm_c_re, ssm_c_im, ssm_d, w_glu, ssm_norm_g, w_out,
           ln1_g, ln1_b, w_router, b_router, w_gate_up, b_gate_up, w_down, b_down, w_ple_gate,
           w_ple_proj, ln2_g, ln2_b):
    B, L, D = x.shape
    T = B * L
    assert D == SUBLANES * LANES, "token-slab layout holds one token per (8, 128) tile"
    for i in range(DEPTH):
        lambda_init = 0.8 - 0.6 * math.exp(-0.3 * i)
        q, kt, v, u = _in_proj(x, w_in[i])
        lam_vecs = jnp.stack([lambda_q1[i], lambda_k1[i], lambda_q2[i], lambda_k2[i]]).astype(F32)
        attn = _attention(q, kt, v, lam_vecs, subln_g[i], lambda_init)
        b_fold, c_fold, a_tile = _ssm_fold_params(ssm_a_re[i], ssm_a_im[i], ssm_log_dt[i],
                                                  ssm_b_re[i], ssm_b_im[i], ssm_c_re[i],
                                                  ssm_c_im[i], B)
        ssm = _ssm(u, b_fold, c_fold, a_tile, ssm_d[i], w_glu[i], ssm_norm_g[i])

        x1, idx, gates, counts = _mix_route(
            x.reshape(T, D), attn.reshape(T, -1), ssm.reshape(T, -1), w_out[i], ln1_g[i],
            ln1_b[i], w_router[i], b_router[i])

        n_assign = T * TOP_K
        cnt = counts[:, 0].astype(jnp.int32)
        n_blk_e = (cnt + MOE_ROWS - 1) // MOE_ROWS
        fill_e = n_blk_e * MOE_ROWS - cnt
        e_ids = jnp.arange(N_EXPERTS, dtype=jnp.int32)[:, None]
        fill_key = jnp.where(jnp.arange(MOE_ROWS, dtype=jnp.int32)[None, :] < fill_e[:, None],
                             e_ids, N_EXPERTS)
        keys = jnp.concatenate([idx.reshape(-1), fill_key.reshape(-1)])
        order = jnp.argsort(keys, stable=True).astype(jnp.int32)
        real = order < n_assign
        gsrc = jnp.where(real, lax.rem(order, T), 0) * SUBLANES
        sdst = jnp.where(real, order, 0) * SUBLANES
        blk_end = jnp.cumsum(n_blk_e)
        n_blocks = n_assign // MOE_ROWS + N_EXPERTS
        blk = jnp.arange(n_blocks, dtype=jnp.int32)
        block_e = jnp.minimum(jnp.sum(blk_end[None, :] <= blk[:, None], axis=1),
                              N_EXPERTS - 1).astype(jnp.int32)
        local = blk - (blk_end - n_blk_e)[block_e]
        n_used = blk_end[-1:].astype(jnp.int32)
        blk_cnt = jnp.clip(cnt[block_e] - local * MOE_ROWS, 0, MOE_ROWS).astype(jnp.int32)
        blk_cnt = jnp.where(blk < n_used, blk_cnt, 0)

        out4 = _experts(x1, gsrc, sdst, block_e, n_used, blk_cnt, w_gate_up[i], b_gate_up[i],
                        w_down[i], b_down[i])
        x = _combine(gates.T, x1, out4.reshape(TOP_K, T * SUBLANES, LANES), p[i].reshape(T, -1), w_ple_gate[i], w_ple_proj[i],
                     ln2_g[i], ln2_b[i]).reshape(B, L, D)
    return x
```

```python
import functools
import math

import jax
import jax.numpy as jnp
from jax import lax
from jax.experimental import pallas as pl
from jax.experimental.pallas import tpu as pltpu

F32 = jnp.float32
BF16 = jnp.bfloat16

N_HEADS = 8
HEAD_DIM = 32
V_DIM = 2 * HEAD_DIM
D_ATTN = N_HEADS * V_DIM
SSM_GROUP = 16
SSM_STATE = 64
N_EXPERTS = 32
TOP_K = 4
TOP_K_SHIFT = 2
SWIGLU_LIMIT = 7.0
SWIGLU_ALPHA = 1.702
LN_EPS = 1e-5
RMS_EPS = 1e-5
DEPTH = 1
DEEPNORM_ALPHA = (2.0 * DEPTH) ** 0.25
LOG2E = math.log2(math.e)

LANES = 128
SUBLANES = 8
VMEM_LIMIT = 56 * 1024 * 1024

PROJ_ROWS = 512
ATTN_Q = 512
BIAS_PARTS = 3
SSM_CHUNK = 128
SSM_MM_ROWS = 256
MIX_ROWS = 512
MOE_ROWS = 256
SPLIT_COLS = 512
WPREP_ROWS = 256
IDX_WIN = 2048
IDX_ALIGN = 1024
IDX_SHIFT = 10
COMBINE_ROWS = 512


def _params(*sem):
    return pltpu.CompilerParams(dimension_semantics=sem, vmem_limit_bytes=VMEM_LIMIT)


def _in_proj_kernel(x_ref, wq_ref, wkt_ref, wv_ref, wu_ref, q_ref, kt_ref, v_ref, u_ref):
    xb = x_ref[0].astype(BF16)
    q = jnp.dot(xb, wq_ref[...], preferred_element_type=F32)
    q_ref[0] = (q * (HEAD_DIM ** -0.5 * LOG2E)).astype(BF16)
    kt = lax.dot_general(wkt_ref[...], xb, (((1,), (1,)), ((), ())),
                         preferred_element_type=F32)
    kt_ref[0, 0] = kt.astype(BF16)
    v = jnp.dot(xb, wv_ref[...], preferred_element_type=F32)
    lane = lax.broadcasted_iota(jnp.int32, (1, LANES), 1)
    pad = (lane == V_DIM).astype(F32)
    blocks = []
    for j in range(N_HEADS // 2):
        pair = v[:, j * LANES:(j + 1) * LANES]
        blocks.append(jnp.where(lane < V_DIM, pair, pad))
        blocks.append(jnp.where(lane < V_DIM, pltpu.roll(pair, V_DIM, 1), pad))
    v_ref[0] = jnp.concatenate(blocks, axis=1).astype(BF16)
    u_ref[0] = jnp.dot(xb, wu_ref[...], preferred_element_type=F32)


def _in_proj(x, w_in):
    B, L, D = x.shape
    d_ssm = w_in.shape[1] - 3 * D_ATTN
    tm = PROJ_ROWS
    nt = L // tm
    wb = w_in.astype(BF16)
    wq = wb[:, :D_ATTN]
    wkt = wb[:, D_ATTN:2 * D_ATTN].T
    wv = wb[:, 2 * D_ATTN:3 * D_ATTN]
    wu = wb[:, 3 * D_ATTN:]
    full = lambda shape: pl.BlockSpec(shape, lambda b, i: (0,) * len(shape))
    return pl.pallas_call(
        _in_proj_kernel,
        grid=(B, nt),
        in_specs=[pl.BlockSpec((1, tm, D), lambda b, i: (b, i, 0)),
                  full((D, D_ATTN)), full((D_ATTN, D)), full((D, D_ATTN)), full((D, d_ssm))],
        out_specs=[pl.BlockSpec((1, tm, D_ATTN), lambda b, i: (b, i, 0)),
                   pl.BlockSpec((1, 1, D_ATTN, tm), lambda b, i: (b, i, 0, 0)),
                   pl.BlockSpec((1, tm, N_HEADS * LANES), lambda b, i: (b, i, 0)),
                   pl.BlockSpec((1, tm, d_ssm), lambda b, i: (b, i, 0))],
        out_shape=[jax.ShapeDtypeStruct((B, L, D_ATTN), BF16),
                   jax.ShapeDtypeStruct((B, nt, D_ATTN, tm), BF16),
                   jax.ShapeDtypeStruct((B, L, N_HEADS * LANES), BF16),
                   jax.ShapeDtypeStruct((B, L, d_ssm), F32)],
        compiler_params=_params("parallel", "parallel"),
        name="in_proj",
    )(x, wq, wkt, wv, wu)


def _attn_kernel(slope_ref, lam_ref, q_ref, kt_ref, v_ref, g_ref, o_ref,
                 q4_ref, jr_ref, p_ref, m_ref, acc_ref, *, tq, tk, lambda_init):
    hp = pl.program_id(1)
    qi = pl.program_id(2)
    q0 = qi * tq
    n_cb = tk // LANES

    q = q_ref[0]
    lane_q = lax.broadcasted_iota(jnp.int32, q.shape, 1)
    for c in range(4):
        rs = slice(c * tq, (c + 1) * tq)
        lo = (c // 2) * BIAS_PARTS
        q4_ref[rs, :LANES] = jnp.where(lane_q // HEAD_DIM == c, q, jnp.zeros_like(q))
        q4_ref[rs, LANES:] = ((lane_q >= lo) & (lane_q < lo + BIAS_PARTS)).astype(BF16)

    @pl.when(qi == 0)
    def _():
        colf = lax.broadcasted_iota(jnp.int32, (1, tk), 1).astype(F32)
        r_i = lax.broadcasted_iota(jnp.int32, (LANES, tk), 0)
        jr = jnp.zeros((LANES, tk), F32)
        for h in range(2):
            rem = (slope_ref[2 * hp + h] * LOG2E) * colf
            for part in range(BIAS_PARTS):
                piece = rem.astype(BF16).astype(F32)
                jr = jnp.where(r_i == h * BIAS_PARTS + part, piece, jr)
                rem = rem - piece
        jr_ref[...] = jr.astype(BF16)

    m_ref[...] = jnp.full(m_ref.shape, -jnp.inf, F32)
    acc_ref[...] = jnp.zeros(acc_ref.shape, F32)

    col = lax.broadcasted_iota(jnp.int32, (1, tk), 1)
    row = lax.broadcasted_iota(jnp.int32, (tq, 1), 0)
    zero_row = jnp.zeros((1, LANES), jnp.int32)

    def v_tile(t):
        return v_ref[0, pl.ds(pl.multiple_of(t * tk, tk), tk), :]

    def tile(t, masked):
        kt_aug = jnp.concatenate([kt_ref[0, t], jr_ref[...]], axis=0)
        s_all = jnp.dot(q4_ref[...], kt_aug, preferred_element_type=F32)
        rel = t * tk - q0 + col
        base = (t * tk - q0 + zero_row).astype(F32)
        for c in range(4):
            rs = slice(c * tq, (c + 1) * tq)
            off = (slope_ref[2 * hp + c // 2] * LOG2E) * base
            s = s_all[rs]
            if masked:
                s = jnp.where(rel <= row, s, -jnp.inf)
            blocks = [s[:, cb * LANES:(cb + 1) * LANES] for cb in range(n_cb)]
            part = blocks[0]
            for blk in blocks[1:]:
                part = jnp.maximum(part, blk)
            m_old = m_ref[rs]
            m_new = jnp.maximum(m_old, jnp.max(part, axis=1, keepdims=True) + off)
            m_ref[rs] = m_new
            alpha = jnp.exp2(m_old - m_new)
            shift = m_new - off
            acc_ref[rs] = alpha * acc_ref[rs]
            p_ref[rs] = jnp.concatenate([jnp.exp2((blk - shift).astype(BF16)) for blk in blocks],
                                        axis=1)
        v2 = v_tile(t)
        for j in range(2):
            hr = slice(2 * j * tq, (2 * j + 2) * tq)
            acc_ref[hr] += jnp.dot(p_ref[hr], v2[:, j * LANES:(j + 1) * LANES],
                                   preferred_element_type=F32)

    n_full = q0 // tk

    def body(t, carry):
        tile(t, False)
        return carry

    lax.fori_loop(0, n_full, body, 0)
    tile(n_full, True)
    acc = acc_ref[...]

    lam = lam_ref[...]
    lane = lax.broadcasted_iota(jnp.int32, (1, LANES), 1)
    outs = []
    for j in range(2):
        o = []
        for c in (2 * j, 2 * j + 1):
            a = acc[c * tq:(c + 1) * tq]
            denom = jnp.sum(jnp.where(lane == V_DIM, a, 0.0), axis=1, keepdims=True)
            o.append(a / denom)
        d = o[0] - lam * o[1]
        ms = jnp.sum(jnp.where(lane < V_DIM, d * d, 0.0), axis=1, keepdims=True) * (1.0 / V_DIM)
        outs.append(d * lax.rsqrt(ms + RMS_EPS))
    out = jnp.where(lane < V_DIM, outs[0], pltpu.roll(outs[1], V_DIM, 1))
    out = out * g_ref[...] * (1.0 - lambda_init)
    o_ref[0] = out.astype(o_ref.dtype)


def _attention(q, kt, v, lam_vecs, subln_g, lambda_init):
    B, L, _ = q.shape
    tq, tk = ATTN_Q, PROJ_ROWS
    nk = L // tk
    slopes = jnp.exp2(-(jnp.arange(N_HEADS, dtype=F32) + 1.0) * (8.0 / N_HEADS))
    lam = (jnp.exp(jnp.sum(lam_vecs[0] * lam_vecs[1])) - jnp.exp(jnp.sum(lam_vecs[2] * lam_vecs[3]))
           + lambda_init).reshape(1, 1).astype(F32)
    g2 = jnp.tile(subln_g.astype(F32), 2).reshape(1, LANES)
    kern = functools.partial(_attn_kernel, tq=tq, tk=tk, lambda_init=lambda_init)
    return pl.pallas_call(
        kern,
        grid_spec=pltpu.PrefetchScalarGridSpec(
            num_scalar_prefetch=1,
            grid=(B, N_HEADS // 2, L // tq),
            in_specs=[pl.BlockSpec((1, 1), lambda b, h, i, s: (0, 0)),
                      pl.BlockSpec((1, tq, LANES), lambda b, h, i, s: (b, i, h)),
                      pl.BlockSpec((1, nk, LANES, tk), lambda b, h, i, s: (b, 0, h, 0)),
                      pl.BlockSpec((1, L, 2 * LANES), lambda b, h, i, s: (b, 0, h)),
                      pl.BlockSpec((1, LANES), lambda b, h, i, s: (0, 0))],
            out_specs=pl.BlockSpec((1, tq, LANES), lambda b, h, i, s: (b, i, h)),
            scratch_shapes=[pltpu.VMEM((4 * tq, 2 * LANES), BF16),
                            pltpu.VMEM((LANES, tk), BF16),
                            pltpu.VMEM((4 * tq, tk), BF16),
                            pltpu.VMEM((4 * tq, LANES), F32),
                            pltpu.VMEM((4 * tq, LANES), F32)]),
        out_shape=jax.ShapeDtypeStruct((B, L, D_ATTN), BF16),
        compiler_params=_params("parallel", "parallel", "arbitrary"),
        name="diff_attn",
    )(slopes, lam, q, kt, v, g2)


def _ssm_fold_params(a_re, a_im, log_dt, b_re, b_im, c_re, c_im, batch):
    G, P = a_re.shape
    C = b_re.shape[-1]
    half = G // 2
    A = lax.complex(a_re.astype(F32), a_im.astype(F32))
    dt = jnp.exp(log_dt.astype(F32))[:, None]
    a_bar = jnp.exp(A * dt)
    b_bar = ((a_bar - 1.0) / A)[..., None] * lax.complex(b_re.astype(F32), b_im.astype(F32))
    sel = jnp.eye(half, dtype=F32)[jnp.arange(G) % half]
    fold_b = lambda m: jnp.einsum('gpc,gk->gckp', m, sel).reshape(G * C, half * P)
    b_fold = jnp.concatenate([fold_b(jnp.real(b_bar)), fold_b(jnp.imag(b_bar))], axis=1)
    fold_c = lambda m: jnp.einsum('gcp,gk->kpgc', m, sel).reshape(half * P, G * C)
    c_fold = jnp.concatenate([fold_c(c_re.astype(F32)), -fold_c(c_im.astype(F32))], axis=0)
    a_rows = jnp.tile(a_bar.reshape(2, half * P), (batch, 1))
    a_tile = jnp.concatenate([jnp.real(a_rows), jnp.imag(a_rows)], axis=1)
    return b_fold.astype(BF16), c_fold.astype(BF16), a_tile.astype(F32)


def _gelu_tanh(x):
    return 0.5 * x * (1.0 + jnp.tanh(math.sqrt(2.0 / math.pi) * (x + 0.044715 * (x * x * x))))


def _ssm_kernel(u_ref, bf_ref, cf_ref, a_ref, d_ref, wglu_ref, gn_ref, o_ref,
                lhs_ref, x_ref, y_ref, s_ref, *, chunk, batch):
    i = pl.program_id(0)
    d_ssm = u_ref.shape[-1]
    n_slab = d_ssm // LANES
    ns = x_ref.shape[-1] // 2
    rows = SUBLANES

    @pl.when(i == 0)
    def _():
        s_ref[...] = jnp.zeros_like(s_ref)

    zero = jnp.zeros((chunk, LANES), F32)
    for b in range(batch):
        ub = u_ref[b]
        for h in range(2):
            for c in range(n_slab):
                src = ub[:, c * LANES:(c + 1) * LANES] if (c * 2) // n_slab == h else zero
                lhs_ref[c, pl.ds(2 * b + h, chunk, stride=rows), :] = src
    n_blk = (rows * chunk) // SSM_MM_ROWS

    hs = ns // 2
    n_set = 2

    def in_mm(rb, carry):
        rs = pl.ds(pl.multiple_of(rb * SSM_MM_ROWS, SSM_MM_ROWS), SSM_MM_ROWS)
        for s in range(n_set):
            lhs = jnp.concatenate([lhs_ref[s, rs, :], lhs_ref[s + n_slab // 2, rs, :]],
                                  axis=1).astype(BF16)
            xs = jnp.dot(lhs, bf_ref[s], preferred_element_type=F32)
            x_ref[rs, s * hs:(s + 1) * hs] = xs[:, :hs]
            x_ref[rs, ns + s * hs:ns + (s + 1) * hs] = xs[:, hs:]
        return carry

    lax.fori_loop(0, n_blk, in_mm, 0)

    a_re = a_ref[:, :ns]
    a_im = a_ref[:, ns:]

    def step(t, carry):
        s_re, s_im = carry
        r0 = pl.multiple_of(t * rows, rows)
        n_re = a_re * s_re - a_im * s_im + x_ref[pl.ds(r0, rows), :ns]
        n_im = a_re * s_im + a_im * s_re + x_ref[pl.ds(r0, rows), ns:]
        x_ref[pl.ds(r0, rows), :ns] = n_re
        x_ref[pl.ds(r0, rows), ns:] = n_im
        return n_re, n_im

    s_re, s_im = lax.fori_loop(0, chunk, step, (s_ref[:, :ns], s_ref[:, ns:]), unroll=2)
    s_ref[:, :ns] = s_re
    s_ref[:, ns:] = s_im

    def out_mm(rb, carry):
        rs = pl.ds(pl.multiple_of(rb * SSM_MM_ROWS, SSM_MM_ROWS), SSM_MM_ROWS)
        for s in range(n_set):
            st = jnp.concatenate([x_ref[rs, s * hs:(s + 1) * hs],
                                  x_ref[rs, ns + s * hs:ns + (s + 1) * hs]], axis=1).astype(BF16)
            y = jnp.dot(st, cf_ref[s], preferred_element_type=F32)
            y_ref[s, rs, :] = y[:, :LANES]
            y_ref[s + n_slab // 2, rs, :] = y[:, LANES:]
        return carry

    lax.fori_loop(0, n_blk, out_mm, 0)

    for b in range(batch):
        parts = []
        for c in range(n_slab):
            h = (c * 2) // n_slab
            parts.append(y_ref[c, pl.ds(2 * b + h, chunk, stride=rows), :])
        yb = jnp.concatenate(parts, axis=1) + d_ref[...] * u_ref[b]
        g = jnp.dot(_gelu_tanh(yb).astype(BF16), wglu_ref[...], preferred_element_type=F32)
        z = g[:, :d_ssm] * jax.nn.sigmoid(g[:, d_ssm:])
        ms = jnp.mean(z * z, axis=1, keepdims=True)
        o_ref[b] = (z * lax.rsqrt(ms + RMS_EPS) * gn_ref[...]).astype(o_ref.dtype)


def _ssm(u, b_fold, c_fold, a_tile, d_skip, w_glu, norm_g):
    B, L, d_ssm = u.shape
    assert 2 * B == SUBLANES, "row-stream layout packs batch x 2 group halves on 8 sublanes"
    chunk = SSM_CHUNK
    ns2 = a_tile.shape[1]
    n_slab = d_ssm // LANES
    assert n_slab == 4, "two group sets x two group halves of 128 channels"
    ns, hs = ns2 // 2, ns2 // 4
    ch = lambda s: jnp.r_[s * LANES:(s + 1) * LANES, (s + 2) * LANES:(s + 3) * LANES]
    st = lambda s: jnp.r_[s * hs:(s + 1) * hs, ns + s * hs:ns + (s + 1) * hs]
    b_sets = jnp.stack([b_fold[ch(s)][:, st(s)] for s in range(2)])
    c_sets = jnp.stack([c_fold[st(s)][:, ch(s)] for s in range(2)])
    full = lambda shape: pl.BlockSpec(shape, lambda i: (0,) * len(shape))
    kern = functools.partial(_ssm_kernel, chunk=chunk, batch=B)
    return pl.pallas_call(
        kern,
        grid=(L // chunk,),
        in_specs=[pl.BlockSpec((B, chunk, d_ssm), lambda i: (0, i, 0)),
                  full(b_sets.shape), full(c_sets.shape), full((SUBLANES, ns2)),
                  full((1, d_ssm)), full((d_ssm, 2 * d_ssm)), full((1, d_ssm))],
        out_specs=pl.BlockSpec((B, chunk, d_ssm), lambda i: (0, i, 0)),
        out_shape=jax.ShapeDtypeStruct((B, L, d_ssm), BF16),
        scratch_shapes=[pltpu.VMEM((n_slab, SUBLANES * chunk, LANES), F32),
                        pltpu.VMEM((SUBLANES * chunk, ns2), F32),
                        pltpu.VMEM((n_slab, SUBLANES * chunk, LANES), F32),
                        pltpu.VMEM((SUBLANES, ns2), F32)],
        compiler_params=_params("arbitrary"),
        name="s5_ssm",
    )(u, b_sets, c_sets, a_tile, d_skip.reshape(1, d_ssm).astype(F32), w_glu.astype(BF16),
      norm_g.reshape(1, d_ssm).astype(F32))


def _layer_norm(x, g, b):
    mu = jnp.mean(x, axis=1, keepdims=True)
    xc = x - mu
    var = jnp.mean(xc * xc, axis=1, keepdims=True)
    return xc * lax.rsqrt(var + LN_EPS) * g + b


def _slab_load(ref, n_rows, d, lead=()):
    return jnp.concatenate(
        [ref[lead + (pl.ds(c, n_rows, stride=SUBLANES), slice(None))] for c in range(d // LANES)],
        axis=1)


def _slab_store(ref, val):
    n_rows, d = val.shape
    for c in range(d // LANES):
        ref[pl.ds(c, n_rows, stride=SUBLANES), :] = val[:, c * LANES:(c + 1) * LANES]


def _mix_kernel(x_ref, attn_ref, ssm_ref, woa_ref, wos_ref, g_ref, b_ref, wrt_ref, br_ref,
                x1_ref, idx_ref, gate_ref, cnt_ref, carry_ref):
    i = pl.program_id(0)
    tm = x_ref.shape[0]

    @pl.when(i == 0)
    def _():
        carry_ref[...] = jnp.zeros_like(carry_ref)

    mix = (jnp.dot(attn_ref[...], woa_ref[...], preferred_element_type=F32)
           + jnp.dot(ssm_ref[...], wos_ref[...], preferred_element_type=F32))
    x1 = _layer_norm(DEEPNORM_ALPHA * x_ref[...] + mix, g_ref[...], b_ref[...])
    _slab_store(x1_ref, x1)

    logits = lax.dot_general(wrt_ref[...], x1.astype(BF16), (((1,), (1,)), ((), ())),
                             preferred_element_type=F32) + br_ref[...]
    e_iota = lax.broadcasted_iota(jnp.int32, logits.shape, 0).astype(F32)
    work = logits
    vals, hots = [], []
    for k in range(TOP_K):
        mx = jnp.max(work, axis=0, keepdims=True)
        idx = jnp.min(jnp.where(work == mx, e_iota, float(N_EXPERTS)), axis=0, keepdims=True)
        hot = e_iota == idx
        work = jnp.where(hot, -jnp.inf, work)
        vals.append(mx)
        hots.append(hot)
        idx_ref[k:k + 1, :] = idx.astype(jnp.int32)
    ex = [jnp.exp(v - vals[0]) for v in vals]
    den = ex[0] + ex[1] + ex[2] + ex[3]
    for k in range(TOP_K):
        gate_ref[k:k + 1, :] = ex[k] / den

    sel = (hots[0] | hots[1] | hots[2] | hots[3]).astype(F32)
    carry_ref[...] = carry_ref[...] + jnp.sum(sel, axis=1, keepdims=True)
    cnt_ref[...] = carry_ref[...]


def _mix_route(x2, attn2, ssm2, w_out, ln_g, ln_b, w_router, b_router):
    T, D = x2.shape
    tm = MIX_ROWS
    d_attn = attn2.shape[1]
    wo = w_out.astype(BF16)
    full = lambda shape: pl.BlockSpec(shape, lambda i: (0,) * len(shape))
    rowblk = lambda w: pl.BlockSpec((tm, w), lambda i: (i, 0))
    colblk = pl.BlockSpec((TOP_K, tm), lambda i: (0, i))
    return pl.pallas_call(
        _mix_kernel,
        grid=(T // tm,),
        in_specs=[rowblk(D), rowblk(d_attn), rowblk(ssm2.shape[1]),
                  full((d_attn, D)), full((D - d_attn, D)), full((1, D)), full((1, D)),
                  full((N_EXPERTS, D)), full((N_EXPERTS, 1))],
        out_specs=[pl.BlockSpec((tm * SUBLANES, LANES), lambda i: (i, 0)),
                   colblk, colblk, full((N_EXPERTS, LANES))],
        out_shape=[jax.ShapeDtypeStruct((T * SUBLANES, LANES), F32),
                   jax.ShapeDtypeStruct((TOP_K, T), jnp.int32),
                   jax.ShapeDtypeStruct((TOP_K, T), F32),
                   jax.ShapeDtypeStruct((N_EXPERTS, LANES), F32)],
        scratch_shapes=[pltpu.VMEM((N_EXPERTS, LANES), F32)],
        compiler_params=_params("arbitrary"),
        name="mix_route",
    )(x2, attn2, ssm2, wo[:d_attn], wo[d_attn:], ln_g.reshape(1, D).astype(F32),
      ln_b.reshape(1, D).astype(F32), w_router.T.astype(BF16),
      b_router.reshape(N_EXPERTS, 1).astype(F32))


def _expert_kernel(be_ref, nu_ref, src_ref, cnt_ref, gsrc_ref, sdst_ref, x1_ref, wgu_ref, bg_ref,
                   bl_ref, wd_ref, bd_ref, o4_ref, sel_ref, wgl_ref, wdb_ref, xbuf, obuf, ibuf,
                   gsem, ssem, isem):
    i = pl.program_id(0)
    nu = nu_ref[0]
    _, d, f2 = wgu_ref.shape
    f = f2 // 2
    half = SPLIT_COLS // 2
    active = i < nu
    blk_rows = MOE_ROWS * SUBLANES

    def idx_copies(j):
        base = pl.multiple_of(lax.shift_right_logical(src_ref[j], IDX_SHIFT) * IDX_ALIGN, IDX_ALIGN)
        b = lax.rem(j, 3)
        dst = lambda h: ibuf.at[pl.ds(pl.multiple_of((2 * b + h) * IDX_WIN, IDX_WIN), IDX_WIN)]
        return (pltpu.make_async_copy(gsrc_ref.at[pl.ds(base, IDX_WIN)], dst(0), isem.at[b]),
                pltpu.make_async_copy(sdst_ref.at[pl.ds(base, IDX_WIN)], dst(1), isem.at[b]))

    def entry_base(j, h):
        return (2 * lax.rem(j, 3) + h) * IDX_WIN + (src_ref[j] & (IDX_ALIGN - 1))

    def rows(start, n=SUBLANES):
        if isinstance(start, int):
            return pl.ds(start, n)
        return pl.ds(pl.multiple_of(start, SUBLANES), n)

    def gather_row(slot, ebase, r):
        return pltpu.make_async_copy(x1_ref.at[rows(ibuf[ebase + r])],
                                     xbuf.at[rows(slot * blk_rows + r * SUBLANES)],
                                     gsem.at[slot])

    def scatter_row(slot, ebase, r):
        return pltpu.make_async_copy(obuf.at[rows(slot * blk_rows + r * SUBLANES)],
                                     o4_ref.at[rows(ibuf[ebase + r])], ssem.at[slot])

    def start_block(j, row_copy):
        for slot in range(2):
            @pl.when((j & 1) == slot)
            def _():
                for r in range(MOE_ROWS):
                    row_copy(slot, r).start(priority=r % 2)

    def start_gather(j):
        ebase = entry_base(j, 0)
        start_block(j, lambda slot, r: gather_row(slot, ebase, r))

    def wait_gather(j):
        pltpu.make_async_copy(x1_ref.at[rows(0, blk_rows)],
                              xbuf.at[rows((j & 1) * blk_rows, blk_rows)], gsem.at[j & 1]).wait()

    def start_scatter(j):
        ebase = entry_base(j, 1)

        @pl.when(cnt_ref[j] == MOE_ROWS)
        def _():
            start_block(j, lambda slot, r: scatter_row(slot, ebase, r))

        @pl.when(cnt_ref[j] < MOE_ROWS)
        def _():
            def one(r, c):
                scatter_row(j & 1, ebase, r).start()
                return c

            lax.fori_loop(0, cnt_ref[j], one, 0)

    def wait_scatter(j):
        @pl.when(cnt_ref[j] == MOE_ROWS)
        def _():
            pltpu.make_async_copy(obuf.at[rows((j & 1) * blk_rows, blk_rows)],
                                  o4_ref.at[rows(0, blk_rows)], ssem.at[j & 1]).wait()

        @pl.when(cnt_ref[j] < MOE_ROWS)
        def _():
            def one(r, c):
                pltpu.make_async_copy(obuf.at[rows(0)], o4_ref.at[rows(0)], ssem.at[j & 1]).wait()
                return c

            lax.fori_loop(0, cnt_ref[j], one, 0)

    @pl.when(i == 0)
    def _():
        r = lax.broadcasted_iota(jnp.int32, sel_ref.shape, 0)
        c = lax.broadcasted_iota(jnp.int32, sel_ref.shape, 1)
        src = jnp.where(c < half, 2 * c, 2 * (c - half) + 1)
        sel_ref[...] = (r == src).astype(BF16)
        for cp in idx_copies(0):
            cp.start()
        for cp in idx_copies(0):
            cp.wait()

        @pl.when(nu > 1)
        def _():
            for cp in idx_copies(1):
                cp.start()

        ebase0 = entry_base(0, 0)

        def first(r, c):
            gather_row(0, ebase0, r).start()
            return c

        lax.fori_loop(0, MOE_ROWS, first, 0)

    @pl.when(active)
    def _():
        @pl.when(i + 2 < nu)
        def _():
            for cp in idx_copies(i + 2):
                cp.start()

        wait_gather(i)

        @pl.when(i + 1 < nu)
        def _():
            for cp in idx_copies(i + 1):
                cp.wait()
            start_gather(i + 1)

    new_expert = (i == 0) | (be_ref[i] != be_ref[jnp.maximum(i - 1, 0)])

    @pl.when(active & new_expert)
    def _():
        def split_rows(rb, carry):
            rs = pl.ds(pl.multiple_of(rb * WPREP_ROWS, WPREP_ROWS), WPREP_ROWS)
            for j in range(f2 // SPLIT_COLS):
                w = wgu_ref[0, rs, j * SPLIT_COLS:(j + 1) * SPLIT_COLS].astype(BF16)
                y = jnp.dot(w, sel_ref[...], preferred_element_type=F32).astype(BF16)
                wgl_ref[rs, j * half:(j + 1) * half] = y[:, :half]
                wgl_ref[rs, f + j * half:f + (j + 1) * half] = y[:, half:]
            return carry

        def cast_rows(rb, carry):
            rs = pl.ds(pl.multiple_of(rb * WPREP_ROWS, WPREP_ROWS), WPREP_ROWS)
            wdb_ref[rs, :] = wd_ref[0, rs, :].astype(BF16)
            return carry

        lax.fori_loop(0, d // WPREP_ROWS, split_rows, 0)
        lax.fori_loop(0, f // WPREP_ROWS, cast_rows, 0)

    @pl.when(active)
    def _():
        base = pl.multiple_of((i & 1) * blk_rows, blk_rows)
        xb = _slab_load(xbuf.at[pl.ds(base, blk_rows)], MOE_ROWS, d).astype(BF16)
        h = jnp.dot(xb, wgl_ref[...], preferred_element_type=F32)
        xg = jnp.minimum(h[:, :f] + bg_ref[0], SWIGLU_LIMIT)
        xl = jnp.clip(h[:, f:] + bl_ref[0], -SWIGLU_LIMIT, SWIGLU_LIMIT)
        act = xg * jax.nn.sigmoid(SWIGLU_ALPHA * xg) * (xl + 1.0)
        out = jnp.dot(act.astype(BF16), wdb_ref[...], preferred_element_type=F32) + bd_ref[0]
        _slab_store(obuf.at[pl.ds(base, blk_rows)], out)

        start_scatter(i)

        @pl.when(i >= 1)
        def _():
            wait_scatter(i - 1)

        @pl.when(i == nu - 1)
        def _():
            wait_scatter(i)


def _experts(x1s, gsrc, sdst, block_e, n_used, blk_src, blk_cnt, w_gate_up, b_gate_up, w_down,
             b_down):
    E, D, F2 = w_gate_up.shape
    F = F2 // 2
    tm = MOE_ROWS
    n_blocks = block_e.shape[0]
    bg = b_gate_up[:, 0::2].reshape(E, 1, F).astype(F32)
    bl = b_gate_up[:, 1::2].reshape(E, 1, F).astype(F32)
    bd = b_down.reshape(E, 1, D).astype(F32)
    wmap = lambda i, be, nu, sr, cn: (be[i], 0, 0)
    any_spec = pl.BlockSpec(memory_space=pl.ANY)
    return pl.pallas_call(
        _expert_kernel,
        grid_spec=pltpu.PrefetchScalarGridSpec(
            num_scalar_prefetch=4,
            grid=(n_blocks,),
            in_specs=[any_spec, any_spec, any_spec,
                      pl.BlockSpec((1, D, F2), wmap),
                      pl.BlockSpec((1, 1, F), wmap), pl.BlockSpec((1, 1, F), wmap),
                      pl.BlockSpec((1, F, D), wmap), pl.BlockSpec((1, 1, D), wmap)],
            out_specs=any_spec,
            scratch_shapes=[pltpu.VMEM((SPLIT_COLS, SPLIT_COLS), BF16),
                            pltpu.VMEM((D, F2), BF16),
                            pltpu.VMEM((F, D), BF16),
                            pltpu.VMEM((2 * tm * SUBLANES, LANES), F32),
                            pltpu.VMEM((2 * tm * SUBLANES, LANES), F32),
                            pltpu.SMEM((3 * 2 * IDX_WIN,), jnp.int32),
                            pltpu.SemaphoreType.DMA((2,)), pltpu.SemaphoreType.DMA((2,)),
                            pltpu.SemaphoreType.DMA((3,))]),
        out_shape=jax.ShapeDtypeStruct((TOP_K * x1s.shape[0], LANES), F32),
        compiler_params=_params("arbitrary"),
        name="moe_experts",
    )(block_e, n_used, blk_src, blk_cnt, gsrc, sdst, x1s, w_gate_up, bg, bl, w_down, bd)


def _combine_kernel(gate_ref, x1_ref, o4_ref, p_ref, wpg_ref, wpp_ref, g_ref, b_ref, o_ref):
    tm, d = o_ref.shape
    pp = jnp.dot(p_ref[...].astype(BF16), wpp_ref[...], preferred_element_type=F32)
    gates = gate_ref[...]
    moe = gates[:, 0:1] * _slab_load(o4_ref, tm, d, lead=(0,))
    for k in range(1, TOP_K):
        moe = moe + gates[:, k:k + 1] * _slab_load(o4_ref, tm, d, lead=(k,))
    r = DEEPNORM_ALPHA * _slab_load(x1_ref, tm, d) + moe
    gate = jax.nn.sigmoid(jnp.dot(r.astype(BF16), wpg_ref[...], preferred_element_type=F32))
    o_ref[...] = _layer_norm(r + gate * pp, g_ref[...], b_ref[...])


def _combine(gates_t, x1s, out4, p2, w_ple_gate, w_ple_proj, ln_g, ln_b):
    T, pd = p2.shape
    D = w_ple_gate.shape[0]
    tm = COMBINE_ROWS
    full = lambda shape: pl.BlockSpec(shape, lambda i: (0,) * len(shape))
    return pl.pallas_call(
        _combine_kernel,
        grid=(T // tm,),
        in_specs=[pl.BlockSpec((tm, TOP_K), lambda i: (i, 0)),
                  pl.BlockSpec((tm * SUBLANES, LANES), lambda i: (i, 0)),
                  pl.BlockSpec((TOP_K, tm * SUBLANES, LANES), lambda i: (0, i, 0)),
                  pl.BlockSpec((tm, pd), lambda i: (i, 0)),
                  full((D, D)), full((pd, D)), full((1, D)), full((1, D))],
        out_specs=pl.BlockSpec((tm, D), lambda i: (i, 0)),
        out_shape=jax.ShapeDtypeStruct((T, D), F32),
        compiler_params=_params("parallel"),
        name="moe_combine",
    )(gates_t, x1s, out4, p2, w_ple_gate.astype(BF16), w_ple_proj.astype(BF16),
      ln_g.reshape(1, D).astype(F32), ln_b.reshape(1, D).astype(F32))


def kernel(x, p, w_in, lambda_q1, lambda_k1, lambda_q2, lambda_k2, subln_g, ssm_a_re, ssm_a_im,
           ssm_log_dt, ssm_b_re, ssm_b_im, ssm_c_re, ssm_c_im, ssm_d, w_glu, ssm_norm_g, w_out,
           ln1_g, ln1_b, w_router, b_router, w_gate_up, b_gate_up, w_down, b_down, w_ple_gate,
           w_ple_proj, ln2_g, ln2_b):
    B, L, D = x.shape
    T = B * L
    assert D == SUBLANES * LANES, "token-slab layout holds one token per (8, 128) tile"
    for i in range(DEPTH):
        lambda_init = 0.8 - 0.6 * math.exp(-0.3 * i)
        q, kt, v, u = _in_proj(x, w_in[i])
        lam_vecs = jnp.stack([lambda_q1[i], lambda_k1[i], lambda_q2[i], lambda_k2[i]]).astype(F32)
        attn = _attention(q, kt, v, lam_vecs, subln_g[i], lambda_init)
        b_fold, c_fold, a_tile = _ssm_fold_params(ssm_a_re[i], ssm_a_im[i], ssm_log_dt[i],
                                                  ssm_b_re[i], ssm_b_im[i], ssm_c_re[i],
                                                  ssm_c_im[i], B)
        ssm = _ssm(u, b_fold, c_fold, a_tile, ssm_d[i], w_glu[i], ssm_norm_g[i])

        x1, idx, gates, counts = _mix_route(
            x.reshape(T, D), attn.reshape(T, -1), ssm.reshape(T, -1), w_out[i], ln1_g[i],
            ln1_b[i], w_router[i], b_router[i])

        order = jnp.argsort(idx.T.reshape(-1), stable=True).astype(jnp.int32)
        order = jnp.concatenate([order, jnp.zeros((IDX_WIN,), jnp.int32)])
        tok_row = lax.shift_right_logical(order, TOP_K_SHIFT) * SUBLANES
        gsrc = tok_row
        sdst = (order & (TOP_K - 1)) * (T * SUBLANES) + tok_row
        cnt = counts[:, 0].astype(jnp.int32)
        start = jnp.cumsum(cnt) - cnt
        n_blk_e = (cnt + MOE_ROWS - 1) // MOE_ROWS
        blk_end = jnp.cumsum(n_blk_e)
        n_blocks = (T * TOP_K) // MOE_ROWS + N_EXPERTS
        blk = jnp.arange(n_blocks, dtype=jnp.int32)
        block_e = jnp.minimum(jnp.sum(blk_end[None, :] <= blk[:, None], axis=1),
                              N_EXPERTS - 1).astype(jnp.int32)
        local = blk - (blk_end - n_blk_e)[block_e]
        blk_src = (start[block_e] + local * MOE_ROWS).astype(jnp.int32)
        blk_cnt = jnp.clip(cnt[block_e] - local * MOE_ROWS, 0, MOE_ROWS).astype(jnp.int32)
        n_used = blk_end[-1:].astype(jnp.int32)
        blk_src = jnp.where(blk < n_used, blk_src, 0)
        blk_cnt = jnp.where(blk < n_used, blk_cnt, 0)

        out4 = _experts(x1, gsrc, sdst, block_e, n_used, blk_src, blk_cnt, w_gate_up[i],
                        b_gate_up[i], w_down[i], b_down[i])
        x = _combine(gates.T, x1, out4.reshape(TOP_K, T * SUBLANES, LANES), p[i].reshape(T, -1), w_ple_gate[i], w_ple_proj[i],
                     ln2_g[i], ln2_b[i]).reshape(B, L, D)
    return x
```

```python
import functools
import math

import jax
import jax.numpy as jnp
from jax import lax
from jax.experimental import pallas as pl
from jax.experimental.pallas import tpu as pltpu

F32 = jnp.float32
BF16 = jnp.bfloat16

N_HEADS = 8
HEAD_DIM = 32
V_DIM = 2 * HEAD_DIM
D_ATTN = N_HEADS * V_DIM
SSM_GROUP = 16
SSM_STATE = 64
N_EXPERTS = 32
TOP_K = 4
TOP_K_SHIFT = 2
SWIGLU_LIMIT = 7.0
SWIGLU_ALPHA = 1.702
LN_EPS = 1e-5
RMS_EPS = 1e-5
DEPTH = 1
DEEPNORM_ALPHA = (2.0 * DEPTH) ** 0.25
LOG2E = math.log2(math.e)

LANES = 128
SUBLANES = 8
VMEM_LIMIT = 56 * 1024 * 1024

PROJ_ROWS = 512
ATTN_Q = 512
BIAS_PARTS = 3
SSM_CHUNK = 128
SSM_MM_ROWS = 256
MIX_ROWS = 512
MOE_ROWS = 256
SPLIT_COLS = 512
WPREP_ROWS = 256
IDX_WIN = 2048
IDX_ALIGN = 1024
IDX_SHIFT = 10
COMBINE_ROWS = 512


def _params(*sem):
    return pltpu.CompilerParams(dimension_semantics=sem, vmem_limit_bytes=VMEM_LIMIT)


def _in_proj_kernel(x_ref, wq_ref, wkt_ref, wv_ref, wu_ref, q_ref, kt_ref, v_ref, u_ref):
    xb = x_ref[0].astype(BF16)
    q = jnp.dot(xb, wq_ref[...], preferred_element_type=F32)
    q_ref[0] = (q * (HEAD_DIM ** -0.5 * LOG2E)).astype(BF16)
    kt = lax.dot_general(wkt_ref[...], xb, (((1,), (1,)), ((), ())),
                         preferred_element_type=F32)
    kt_ref[0, 0] = kt.astype(BF16)
    v = jnp.dot(xb, wv_ref[...], preferred_element_type=F32)
    lane = lax.broadcasted_iota(jnp.int32, (1, LANES), 1)
    pad = (lane == V_DIM).astype(F32)
    blocks = []
    for j in range(N_HEADS // 2):
        pair = v[:, j * LANES:(j + 1) * LANES]
        blocks.append(jnp.where(lane < V_DIM, pair, pad))
        blocks.append(jnp.where(lane < V_DIM, pltpu.roll(pair, V_DIM, 1), pad))
    v_ref[0] = jnp.concatenate(blocks, axis=1).astype(BF16)
    u_ref[0] = jnp.dot(xb, wu_ref[...], preferred_element_type=F32)


def _in_proj(x, w_in):
    B, L, D = x.shape
    d_ssm = w_in.shape[1] - 3 * D_ATTN
    tm = PROJ_ROWS
    nt = L // tm
    wb = w_in.astype(BF16)
    wq = wb[:, :D_ATTN]
    wkt = wb[:, D_ATTN:2 * D_ATTN].T
    wv = wb[:, 2 * D_ATTN:3 * D_ATTN]
    wu = wb[:, 3 * D_ATTN:]
    full = lambda shape: pl.BlockSpec(shape, lambda b, i: (0,) * len(shape))
    return pl.pallas_call(
        _in_proj_kernel,
        grid=(B, nt),
        in_specs=[pl.BlockSpec((1, tm, D), lambda b, i: (b, i, 0)),
                  full((D, D_ATTN)), full((D_ATTN, D)), full((D, D_ATTN)), full((D, d_ssm))],
        out_specs=[pl.BlockSpec((1, tm, D_ATTN), lambda b, i: (b, i, 0)),
                   pl.BlockSpec((1, 1, D_ATTN, tm), lambda b, i: (b, i, 0, 0)),
                   pl.BlockSpec((1, tm, N_HEADS * LANES), lambda b, i: (b, i, 0)),
                   pl.BlockSpec((1, tm, d_ssm), lambda b, i: (b, i, 0))],
        out_shape=[jax.ShapeDtypeStruct((B, L, D_ATTN), BF16),
                   jax.ShapeDtypeStruct((B, nt, D_ATTN, tm), BF16),
                   jax.ShapeDtypeStruct((B, L, N_HEADS * LANES), BF16),
                   jax.ShapeDtypeStruct((B, L, d_ssm), F32)],
        compiler_params=_params("parallel", "parallel"),
        name="in_proj",
    )(x, wq, wkt, wv, wu)


def _attn_kernel(slope_ref, lam_ref, q_ref, kt_ref, v_ref, g_ref, o_ref,
                 q4_ref, jr_ref, p_ref, m_ref, acc_ref, *, tq, tk, lambda_init):
    hp = pl.program_id(1)
    qi = pl.program_id(2)
    q0 = qi * tq
    n_cb = tk // LANES

    q = q_ref[0]
    lane_q = lax.broadcasted_iota(jnp.int32, q.shape, 1)
    for c in range(4):
        rs = slice(c * tq, (c + 1) * tq)
        lo = (c // 2) * BIAS_PARTS
        q4_ref[rs, :LANES] = jnp.where(lane_q // HEAD_DIM == c, q, jnp.zeros_like(q))
        q4_ref[rs, LANES:] = ((lane_q >= lo) & (lane_q < lo + BIAS_PARTS)).astype(BF16)

    @pl.when(qi == 0)
    def _():
        colf = lax.broadcasted_iota(jnp.int32, (1, tk), 1).astype(F32)
        r_i = lax.broadcasted_iota(jnp.int32, (LANES, tk), 0)
        jr = jnp.zeros((LANES, tk), F32)
        for h in range(2):
            rem = (slope_ref[2 * hp + h] * LOG2E) * colf
            for part in range(BIAS_PARTS):
                piece = rem.astype(BF16).astype(F32)
                jr = jnp.where(r_i == h * BIAS_PARTS + part, piece, jr)
                rem = rem - piece
        jr_ref[...] = jr.astype(BF16)

    m_ref[...] = jnp.full(m_ref.shape, -jnp.inf, F32)
    acc_ref[...] = jnp.zeros(acc_ref.shape, F32)

    col = lax.broadcasted_iota(jnp.int32, (1, tk), 1)
    row = lax.broadcasted_iota(jnp.int32, (tq, 1), 0)
    zero_row = jnp.zeros((1, LANES), jnp.int32)

    def v_tile(t):
        return v_ref[0, pl.ds(pl.multiple_of(t * tk, tk), tk), :]

    def tile(t, masked):
        kt_aug = jnp.concatenate([kt_ref[0, t], jr_ref[...]], axis=0)
        s_all = jnp.dot(q4_ref[...], kt_aug, preferred_element_type=F32)
        rel = t * tk - q0 + col
        base = (t * tk - q0 + zero_row).astype(F32)
        for c in range(4):
            rs = slice(c * tq, (c + 1) * tq)
            off = (slope_ref[2 * hp + c // 2] * LOG2E) * base
            s = s_all[rs]
            if masked:
                s = jnp.where(rel <= row, s, -jnp.inf)
            blocks = [s[:, cb * LANES:(cb + 1) * LANES] for cb in range(n_cb)]
            part = blocks[0]
            for blk in blocks[1:]:
                part = jnp.maximum(part, blk)
            m_old = m_ref[rs]
            m_new = jnp.maximum(m_old, jnp.max(part, axis=1, keepdims=True) + off)
            m_ref[rs] = m_new
            alpha = jnp.exp2(m_old - m_new)
            shift = m_new - off
            acc_ref[rs] = alpha * acc_ref[rs]
            p_ref[rs] = jnp.concatenate([jnp.exp2((blk - shift).astype(BF16)) for blk in blocks],
                                        axis=1)
        v2 = v_tile(t)
        for j in range(2):
            hr = slice(2 * j * tq, (2 * j + 2) * tq)
            acc_ref[hr] += jnp.dot(p_ref[hr], v2[:, j * LANES:(j + 1) * LANES],
                                   preferred_element_type=F32)

    n_full = q0 // tk

    def body(t, carry):
        tile(t, False)
        return carry

    lax.fori_loop(0, n_full, body, 0)
    tile(n_full, True)
    acc = acc_ref[...]

    lam = lam_ref[...]
    lane = lax.broadcasted_iota(jnp.int32, (1, LANES), 1)
    outs = []
    for j in range(2):
        o = []
        for c in (2 * j, 2 * j + 1):
            a = acc[c * tq:(c + 1) * tq]
            denom = jnp.sum(jnp.where(lane == V_DIM, a, 0.0), axis=1, keepdims=True)
            o.append(a / denom)
        d = o[0] - lam * o[1]
        ms = jnp.sum(jnp.where(lane < V_DIM, d * d, 0.0), axis=1, keepdims=True) * (1.0 / V_DIM)
        outs.append(d * lax.rsqrt(ms + RMS_EPS))
    out = jnp.where(lane < V_DIM, outs[0], pltpu.roll(outs[1], V_DIM, 1))
    out = out * g_ref[...] * (1.0 - lambda_init)
    o_ref[0] = out.astype(o_ref.dtype)


def _attention(q, kt, v, lam_vecs, subln_g, lambda_init):
    B, L, _ = q.shape
    tq, tk = ATTN_Q, PROJ_ROWS
    nk = L // tk
    slopes = jnp.exp2(-(jnp.arange(N_HEADS, dtype=F32) + 1.0) * (8.0 / N_HEADS))
    lam = (jnp.exp(jnp.sum(lam_vecs[0] * lam_vecs[1])) - jnp.exp(jnp.sum(lam_vecs[2] * lam_vecs[3]))
           + lambda_init).reshape(1, 1).astype(F32)
    g2 = jnp.tile(subln_g.astype(F32), 2).reshape(1, LANES)
    kern = functools.partial(_attn_kernel, tq=tq, tk=tk, lambda_init=lambda_init)
    return pl.pallas_call(
        kern,
        grid_spec=pltpu.PrefetchScalarGridSpec(
            num_scalar_prefetch=1,
            grid=(B, N_HEADS // 2, L // tq),
            in_specs=[pl.BlockSpec((1, 1), lambda b, h, i, s: (0, 0)),
                      pl.BlockSpec((1, tq, LANES), lambda b, h, i, s: (b, i, h)),
                      pl.BlockSpec((1, nk, LANES, tk), lambda b, h, i, s: (b, 0, h, 0)),
                      pl.BlockSpec((1, L, 2 * LANES), lambda b, h, i, s: (b, 0, h)),
                      pl.BlockSpec((1, LANES), lambda b, h, i, s: (0, 0))],
            out_specs=pl.BlockSpec((1, tq, LANES), lambda b, h, i, s: (b, i, h)),
            scratch_shapes=[pltpu.VMEM((4 * tq, 2 * LANES), BF16),
                            pltpu.VMEM((LANES, tk), BF16),
                            pltpu.VMEM((4 * tq, tk), BF16),
                            pltpu.VMEM((4 * tq, LANES), F32),
                            pltpu.VMEM((4 * tq, LANES), F32)]),
        out_shape=jax.ShapeDtypeStruct((B, L, D_ATTN), BF16),
        compiler_params=_params("parallel", "parallel", "arbitrary"),
        name="diff_attn",
    )(slopes, lam, q, kt, v, g2)


def _ssm_fold_params(a_re, a_im, log_dt, b_re, b_im, c_re, c_im, batch):
    G, P = a_re.shape
    C = b_re.shape[-1]
    half = G // 2
    A = lax.complex(a_re.astype(F32), a_im.astype(F32))
    dt = jnp.exp(log_dt.astype(F32))[:, None]
    a_bar = jnp.exp(A * dt)
    b_bar = ((a_bar - 1.0) / A)[..., None] * lax.complex(b_re.astype(F32), b_im.astype(F32))
    sel = jnp.eye(half, dtype=F32)[jnp.arange(G) % half]
    fold_b = lambda m: jnp.einsum('gpc,gk->gckp', m, sel).reshape(G * C, half * P)
    b_fold = jnp.concatenate([fold_b(jnp.real(b_bar)), fold_b(jnp.imag(b_bar))], axis=1)
    fold_c = lambda m: jnp.einsum('gcp,gk->kpgc', m, sel).reshape(half * P, G * C)
    c_fold = jnp.concatenate([fold_c(c_re.astype(F32)), -fold_c(c_im.astype(F32))], axis=0)
    a_rows = jnp.tile(a_bar.reshape(2, half * P), (batch, 1))
    a_tile = jnp.concatenate([jnp.real(a_rows), jnp.imag(a_rows)], axis=1)
    return b_fold.astype(BF16), c_fold.astype(BF16), a_tile.astype(F32)


def _gelu_tanh(x):
    return 0.5 * x * (1.0 + jnp.tanh(math.sqrt(2.0 / math.pi) * (x + 0.044715 * (x * x * x))))


def _ssm_kernel(u_ref, bf_ref, cf_ref, a_ref, d_ref, wglu_ref, gn_ref, o_ref,
                lhs_ref, x_ref, y_ref, s_ref, *, chunk, batch):
    i = pl.program_id(0)
    d_ssm = u_ref.shape[-1]
    n_slab = d_ssm // LANES
    ns = x_ref.shape[-1] // 2
    rows = SUBLANES

    @pl.when(i == 0)
    def _():
        s_ref[...] = jnp.zeros_like(s_ref)

    zero = jnp.zeros((chunk, LANES), F32)
    for b in range(batch):
        ub = u_ref[b]
        for h in range(2):
            for c in range(n_slab):
                src = ub[:, c * LANES:(c + 1) * LANES] if (c * 2) // n_slab == h else zero
                lhs_ref[c, pl.ds(2 * b + h, chunk, stride=rows), :] = src
    n_blk = (rows * chunk) // SSM_MM_ROWS

    hs = ns // 2
    n_set = 2

    def in_mm(rb, carry):
        rs = pl.ds(pl.multiple_of(rb * SSM_MM_ROWS, SSM_MM_ROWS), SSM_MM_ROWS)
        for s in range(n_set):
            lhs = jnp.concatenate([lhs_ref[s, rs, :], lhs_ref[s + n_slab // 2, rs, :]],
                                  axis=1).astype(BF16)
            xs = jnp.dot(lhs, bf_ref[s], preferred_element_type=F32)
            x_ref[rs, s * hs:(s + 1) * hs] = xs[:, :hs]
            x_ref[rs, ns + s * hs:ns + (s + 1) * hs] = xs[:, hs:]
        return carry

    lax.fori_loop(0, n_blk, in_mm, 0)

    a_re = a_ref[:, :ns]
    a_im = a_ref[:, ns:]

    def step(t, carry):
        s_re, s_im = carry
        r0 = pl.multiple_of(t * rows, rows)
        n_re = a_re * s_re - a_im * s_im + x_ref[pl.ds(r0, rows), :ns]
        n_im = a_re * s_im + a_im * s_re + x_ref[pl.ds(r0, rows), ns:]
        x_ref[pl.ds(r0, rows), :ns] = n_re
        x_ref[pl.ds(r0, rows), ns:] = n_im
        return n_re, n_im

    s_re, s_im = lax.fori_loop(0, chunk, step, (s_ref[:, :ns], s_ref[:, ns:]), unroll=2)
    s_ref[:, :ns] = s_re
    s_ref[:, ns:] = s_im

    def out_mm(rb, carry):
        rs = pl.ds(pl.multiple_of(rb * SSM_MM_ROWS, SSM_MM_ROWS), SSM_MM_ROWS)
        for s in range(n_set):
            st = jnp.concatenate([x_ref[rs, s * hs:(s + 1) * hs],
                                  x_ref[rs, ns + s * hs:ns + (s + 1) * hs]], axis=1).astype(BF16)
            y = jnp.dot(st, cf_ref[s], preferred_element_type=F32)
            y_ref[s, rs, :] = y[:, :LANES]
            y_ref[s + n_slab // 2, rs, :] = y[:, LANES:]
        return carry

    lax.fori_loop(0, n_blk, out_mm, 0)

    for b in range(batch):
        parts = []
        for c in range(n_slab):
            h = (c * 2) // n_slab
            parts.append(y_ref[c, pl.ds(2 * b + h, chunk, stride=rows), :])
        yb = jnp.concatenate(parts, axis=1) + d_ref[...] * u_ref[b]
        g = jnp.dot(_gelu_tanh(yb).astype(BF16), wglu_ref[...], preferred_element_type=F32)
        z = g[:, :d_ssm] * jax.nn.sigmoid(g[:, d_ssm:])
        ms = jnp.mean(z * z, axis=1, keepdims=True)
        o_ref[b] = (z * lax.rsqrt(ms + RMS_EPS) * gn_ref[...]).astype(o_ref.dtype)


def _ssm(u, b_fold, c_fold, a_tile, d_skip, w_glu, norm_g):
    B, L, d_ssm = u.shape
    assert 2 * B == SUBLANES, "row-stream layout packs batch x 2 group halves on 8 sublanes"
    chunk = SSM_CHUNK
    ns2 = a_tile.shape[1]
    n_slab = d_ssm // LANES
    assert n_slab == 4, "two group sets x two group halves of 128 channels"
    ns, hs = ns2 // 2, ns2 // 4
    ch = lambda s: jnp.r_[s * LANES:(s + 1) * LANES, (s + 2) * LANES:(s + 3) * LANES]
    st = lambda s: jnp.r_[s * hs:(s + 1) * hs, ns + s * hs:ns + (s + 1) * hs]
    b_sets = jnp.stack([b_fold[ch(s)][:, st(s)] for s in range(2)])
    c_sets = jnp.stack([c_fold[st(s)][:, ch(s)] for s in range(2)])
    full = lambda shape: pl.BlockSpec(shape, lambda i: (0,) * len(shape))
    kern = functools.partial(_ssm_kernel, chunk=chunk, batch=B)
    return pl.pallas_call(
        kern,
        grid=(L // chunk,),
        in_specs=[pl.BlockSpec((B, chunk, d_ssm), lambda i: (0, i, 0)),
                  full(b_sets.shape), full(c_sets.shape), full((SUBLANES, ns2)),
                  full((1, d_ssm)), full((d_ssm, 2 * d_ssm)), full((1, d_ssm))],
        out_specs=pl.BlockSpec((B, chunk, d_ssm), lambda i: (0, i, 0)),
        out_shape=jax.ShapeDtypeStruct((B, L, d_ssm), BF16),
        scratch_shapes=[pltpu.VMEM((n_slab, SUBLANES * chunk, LANES), F32),
                        pltpu.VMEM((SUBLANES * chunk, ns2), F32),
                        pltpu.VMEM((n_slab, SUBLANES * chunk, LANES), F32),
                        pltpu.VMEM((SUBLANES, ns2), F32)],
        compiler_params=_params("arbitrary"),
        name="s5_ssm",
    )(u, b_sets, c_sets, a_tile, d_skip.reshape(1, d_ssm).astype(F32), w_glu.astype(BF16),
      norm_g.reshape(1, d_ssm).astype(F32))


def _layer_norm(x, g, b):
    mu = jnp.mean(x, axis=1, keepdims=True)
    xc = x - mu
    var = jnp.mean(xc * xc, axis=1, keepdims=True)
    return xc * lax.rsqrt(var + LN_EPS) * g + b


def _slab_load(ref, n_rows, d, lead=()):
    return jnp.concatenate(
        [ref[lead + (pl.ds(c, n_rows, stride=SUBLANES), slice(None))] for c in range(d // LANES)],
        axis=1)


def _slab_store(ref, val):
    n_rows, d = val.shape
    for c in range(d // LANES):
        ref[pl.ds(c, n_rows, stride=SUBLANES), :] = val[:, c * LANES:(c + 1) * LANES]


def _mix_kernel(x_ref, attn_ref, ssm_ref, woa_ref, wos_ref, g_ref, b_ref, wrt_ref, br_ref,
                x1_ref, idx_ref, gate_ref, cnt_ref, carry_ref):
    i = pl.program_id(0)
    tm = x_ref.shape[0]

    @pl.when(i == 0)
    def _():
        carry_ref[...] = jnp.zeros_like(carry_ref)

    mix = (jnp.dot(attn_ref[...], woa_ref[...], preferred_element_type=F32)
           + jnp.dot(ssm_ref[...], wos_ref[...], preferred_element_type=F32))
    x1 = _layer_norm(DEEPNORM_ALPHA * x_ref[...] + mix, g_ref[...], b_ref[...])
    _slab_store(x1_ref, x1)

    logits = lax.dot_general(wrt_ref[...], x1.astype(BF16), (((1,), (1,)), ((), ())),
                             preferred_element_type=F32) + br_ref[...]
    e_iota = lax.broadcasted_iota(jnp.int32, logits.shape, 0).astype(F32)
    work = logits
    vals, hots = [], []
    for k in range(TOP_K):
        mx = jnp.max(work, axis=0, keepdims=True)
        idx = jnp.min(jnp.where(work == mx, e_iota, float(N_EXPERTS)), axis=0, keepdims=True)
        hot = e_iota == idx
        work = jnp.where(hot, -jnp.inf, work)
        vals.append(mx)
        hots.append(hot)
        idx_ref[k:k + 1, :] = idx.astype(jnp.int32)
    ex = [jnp.exp(v - vals[0]) for v in vals]
    den = ex[0] + ex[1] + ex[2] + ex[3]
    for k in range(TOP_K):
        gate_ref[k:k + 1, :] = ex[k] / den

    sel = (hots[0] | hots[1] | hots[2] | hots[3]).astype(F32)
    carry_ref[...] = carry_ref[...] + jnp.sum(sel, axis=1, keepdims=True)
    cnt_ref[...] = carry_ref[...]


def _mix_route(x2, attn2, ssm2, w_out, ln_g, ln_b, w_router, b_router):
    T, D = x2.shape
    tm = MIX_ROWS
    d_attn = attn2.shape[1]
    wo = w_out.astype(BF16)
    full = lambda shape: pl.BlockSpec(shape, lambda i: (0,) * len(shape))
    rowblk = lambda w: pl.BlockSpec((tm, w), lambda i: (i, 0))
    colblk = pl.BlockSpec((TOP_K, tm), lambda i: (0, i))
    return pl.pallas_call(
        _mix_kernel,
        grid=(T // tm,),
        in_specs=[rowblk(D), rowblk(d_attn), rowblk(ssm2.shape[1]),
                  full((d_attn, D)), full((D - d_attn, D)), full((1, D)), full((1, D)),
                  full((N_EXPERTS, D)), full((N_EXPERTS, 1))],
        out_specs=[pl.BlockSpec((tm * SUBLANES, LANES), lambda i: (i, 0)),
                   colblk, colblk, full((N_EXPERTS, LANES))],
        out_shape=[jax.ShapeDtypeStruct((T * SUBLANES, LANES), F32),
                   jax.ShapeDtypeStruct((TOP_K, T), jnp.int32),
                   jax.ShapeDtypeStruct((TOP_K, T), F32),
                   jax.ShapeDtypeStruct((N_EXPERTS, LANES), F32)],
        scratch_shapes=[pltpu.VMEM((N_EXPERTS, LANES), F32)],
        compiler_params=_params("arbitrary"),
        name="mix_route",
    )(x2, attn2, ssm2, wo[:d_attn], wo[d_attn:], ln_g.reshape(1, D).astype(F32),
      ln_b.reshape(1, D).astype(F32), w_router.T.astype(BF16),
      b_router.reshape(N_EXPERTS, 1).astype(F32))


def _expert_kernel(be_ref, nu_ref, src_ref, cnt_ref, gsrc_ref, sdst_ref, x1_ref, wgu_ref, bg_ref,
                   bl_ref, wd_ref, bd_ref, o4_ref, sel_ref, wgl_ref, wdb_ref, xbuf, obuf, ibuf,
                   gsem, ssem, isem):
    i = pl.program_id(0)
    nu = nu_ref[0]
    _, d, f2 = wgu_ref.shape
    f = f2 // 2
    half = SPLIT_COLS // 2
    active = i < nu
    blk_rows = MOE_ROWS * SUBLANES

    def idx_copies(j):
        base = pl.multiple_of(lax.shift_right_logical(src_ref[j], IDX_SHIFT) * IDX_ALIGN, IDX_ALIGN)
        b = lax.rem(j, 3)
        dst = lambda h: ibuf.at[pl.ds(pl.multiple_of((2 * b + h) * IDX_WIN, IDX_WIN), IDX_WIN)]
        return (pltpu.make_async_copy(gsrc_ref.at[pl.ds(base, IDX_WIN)], dst(0), isem.at[b]),
                pltpu.make_async_copy(sdst_ref.at[pl.ds(base, IDX_WIN)], dst(1), isem.at[b]))

    def entry_base(j, h):
        return (2 * lax.rem(j, 3) + h) * IDX_WIN + (src_ref[j] & (IDX_ALIGN - 1))

    def rows(start, n=SUBLANES):
        if isinstance(start, int):
            return pl.ds(start, n)
        return pl.ds(pl.multiple_of(start, SUBLANES), n)

    def gather_row(slot, ebase, r):
        return pltpu.make_async_copy(x1_ref.at[rows(ibuf[ebase + r])],
                                     xbuf.at[rows(slot * blk_rows + r * SUBLANES)],
                                     gsem.at[slot])

    def scatter_row(slot, ebase, r):
        return pltpu.make_async_copy(obuf.at[rows(slot * blk_rows + r * SUBLANES)],
                                     o4_ref.at[rows(ibuf[ebase + r])], ssem.at[slot])

    def start_block(j, row_copy):
        for slot in range(2):
            @pl.when((j & 1) == slot)
            def _():
                for r in range(MOE_ROWS):
                    row_copy(slot, r).start(priority=r % 2)

    def start_gather(j):
        ebase = entry_base(j, 0)
        start_block(j, lambda slot, r: gather_row(slot, ebase, r))

    def wait_gather(j):
        pltpu.make_async_copy(x1_ref.at[rows(0, blk_rows)],
                              xbuf.at[rows((j & 1) * blk_rows, blk_rows)], gsem.at[j & 1]).wait()

    def start_scatter(j):
        ebase = entry_base(j, 1)

        @pl.when(cnt_ref[j] == MOE_ROWS)
        def _():
            start_block(j, lambda slot, r: scatter_row(slot, ebase, r))

        @pl.when(cnt_ref[j] < MOE_ROWS)
        def _():
            def one(r, c):
                scatter_row(j & 1, ebase, r).start()
                return c

            lax.fori_loop(0, cnt_ref[j], one, 0)

    def wait_scatter(j):
        @pl.when(cnt_ref[j] == MOE_ROWS)
        def _():
            pltpu.make_async_copy(obuf.at[rows((j & 1) * blk_rows, blk_rows)],
                                  o4_ref.at[rows(0, blk_rows)], ssem.at[j & 1]).wait()

        @pl.when(cnt_ref[j] < MOE_ROWS)
        def _():
            def one(r, c):
                pltpu.make_async_copy(obuf.at[rows(0)], o4_ref.at[rows(0)], ssem.at[j & 1]).wait()
                return c

            lax.fori_loop(0, cnt_ref[j], one, 0)

    @pl.when(i == 0)
    def _():
        r = lax.broadcasted_iota(jnp.int32, sel_ref.shape, 0)
        c = lax.broadcasted_iota(jnp.int32, sel_ref.shape, 1)
        src = jnp.where(c < half, 2 * c, 2 * (c - half) + 1)
        sel_ref[...] = (r == src).astype(BF16)
        for cp in idx_copies(0):
            cp.start()
        for cp in idx_copies(0):
            cp.wait()

        @pl.when(nu > 1)
        def _():
            for cp in idx_copies(1):
                cp.start()

        ebase0 = entry_base(0, 0)

        def first(r, c):
            gather_row(0, ebase0, r).start()
            return c

        lax.fori_loop(0, MOE_ROWS, first, 0)

    @pl.when(active)
    def _():
        @pl.when(i + 2 < nu)
        def _():
            for cp in idx_copies(i + 2):
                cp.start()

        wait_gather(i)

        @pl.when(i + 1 < nu)
        def _():
            for cp in idx_copies(i + 1):
                cp.wait()
            start_gather(i + 1)

    new_expert = (i == 0) | (be_ref[i] != be_ref[jnp.maximum(i - 1, 0)])

    @pl.when(active & new_expert)
    def _():
        def split_rows(rb, carry):
            rs = pl.ds(pl.multiple_of(rb * WPREP_ROWS, WPREP_ROWS), WPREP_ROWS)
            for j in range(f2 // SPLIT_COLS):
                w = wgu_ref[0, rs, j * SPLIT_COLS:(j + 1) * SPLIT_COLS].astype(BF16)
                y = jnp.dot(w, sel_ref[...], preferred_element_type=F32).astype(BF16)
                wgl_ref[rs, j * half:(j + 1) * half] = y[:, :half]
                wgl_ref[rs, f + j * half:f + (j + 1) * half] = y[:, half:]
            return carry

        def cast_rows(rb, carry):
            rs = pl.ds(pl.multiple_of(rb * WPREP_ROWS, WPREP_ROWS), WPREP_ROWS)
            wdb_ref[rs, :] = wd_ref[0, rs, :].astype(BF16)
            return carry

        lax.fori_loop(0, d // WPREP_ROWS, split_rows, 0)
        lax.fori_loop(0, f // WPREP_ROWS, cast_rows, 0)

    @pl.when(active)
    def _():
        base = pl.multiple_of((i & 1) * blk_rows, blk_rows)
        xb = _slab_load(xbuf.at[pl.ds(base, blk_rows)], MOE_ROWS, d).astype(BF16)
        h = jnp.dot(xb, wgl_ref[...], preferred_element_type=F32)
        xg = jnp.minimum(h[:, :f] + bg_ref[0], SWIGLU_LIMIT)
        xl = jnp.clip(h[:, f:] + bl_ref[0], -SWIGLU_LIMIT, SWIGLU_LIMIT)
        act = xg * jax.nn.sigmoid(SWIGLU_ALPHA * xg) * (xl + 1.0)
        out = jnp.dot(act.astype(BF16), wdb_ref[...], preferred_element_type=F32) + bd_ref[0]
        _slab_store(obuf.at[pl.ds(base, blk_rows)], out)

        start_scatter(i)

        @pl.when(i >= 1)
        def _():
            wait_scatter(i - 1)

        @pl.when(i == nu - 1)
        def _():
            wait_scatter(i)


def _experts(x1s, gsrc, sdst, block_e, n_used, blk_src, blk_cnt, w_gate_up, b_gate_up, w_down,
             b_down):
    E, D, F2 = w_gate_up.shape
    F = F2 // 2
    tm = MOE_ROWS
    n_blocks = block_e.shape[0]
    bg = b_gate_up[:, 0::2].reshape(E, 1, F).astype(F32)
    bl = b_gate_up[:, 1::2].reshape(E, 1, F).astype(F32)
    bd = b_down.reshape(E, 1, D).astype(F32)
    wmap = lambda i, be, nu, sr, cn: (be[i], 0, 0)
    any_spec = pl.BlockSpec(memory_space=pl.ANY)
    return pl.pallas_call(
        _expert_kernel,
        grid_spec=pltpu.PrefetchScalarGridSpec(
            num_scalar_prefetch=4,
            grid=(n_blocks,),
            in_specs=[any_spec, any_spec, any_spec,
                      pl.BlockSpec((1, D, F2), wmap),
                      pl.BlockSpec((1, 1, F), wmap), pl.BlockSpec((1, 1, F), wmap),
                      pl.BlockSpec((1, F, D), wmap), pl.BlockSpec((1, 1, D), wmap)],
            out_specs=any_spec,
            scratch_shapes=[pltpu.VMEM((SPLIT_COLS, SPLIT_COLS), BF16),
                            pltpu.VMEM((D, F2), BF16),
                            pltpu.VMEM((F, D), BF16),
                            pltpu.VMEM((2 * tm * SUBLANES, LANES), F32),
                            pltpu.VMEM((2 * tm * SUBLANES, LANES), F32),
                            pltpu.SMEM((3 * 2 * IDX_WIN,), jnp.int32),
                            pltpu.SemaphoreType.DMA((2,)), pltpu.SemaphoreType.DMA((2,)),
                            pltpu.SemaphoreType.DMA((3,))]),
        out_shape=jax.ShapeDtypeStruct((TOP_K * x1s.shape[0], LANES), F32),
        compiler_params=_params("arbitrary"),
        name="moe_experts",
    )(block_e, n_used, blk_src, blk_cnt, gsrc, sdst, x1s, w_gate_up, bg, bl, w_down, bd)


def _combine_kernel(gate_ref, x1_ref, o4_ref, p_ref, wpg_ref, wpp_ref, g_ref, b_ref, o_ref):
    tm, d = o_ref.shape
    pp = jnp.dot(p_ref[...].astype(BF16), wpp_ref[...], preferred_element_type=F32)
    gates = gate_ref[...]
    moe = gates[:, 0:1] * _slab_load(o4_ref, tm, d, lead=(0,))
    for k in range(1, TOP_K):
        moe = moe + gates[:, k:k + 1] * _slab_load(o4_ref, tm, d, lead=(k,))
    r = DEEPNORM_ALPHA * _slab_load(x1_ref, tm, d) + moe
    gate = jax.nn.sigmoid(jnp.dot(r.astype(BF16), wpg_ref[...], preferred_element_type=F32))
    o_ref[...] = _layer_norm(r + gate * pp, g_ref[...], b_ref[...])


def _combine(gates_t, x1s, out4, p2, w_ple_gate, w_ple_proj, ln_g, ln_b):
    T, pd = p2.shape
    D = w_ple_gate.shape[0]
    tm = COMBINE_ROWS
    full = lambda shape: pl.BlockSpec(shape, lambda i: (0,) * len(shape))
    return pl.pallas_call(
        _combine_kernel,
        grid=(T // tm,),
        in_specs=[pl.BlockSpec((tm, TOP_K), lambda i: (i, 0)),
                  pl.BlockSpec((tm * SUBLANES, LANES), lambda i: (i, 0)),
                  pl.BlockSpec((TOP_K, tm * SUBLANES, LANES), lambda i: (0, i, 0)),
                  pl.BlockSpec((tm, pd), lambda i: (i, 0)),
                  full((D, D)), full((pd, D)), full((1, D)), full((1, D))],
        out_specs=pl.BlockSpec((tm, D), lambda i: (i, 0)),
        out_shape=jax.ShapeDtypeStruct((T, D), F32),
        compiler_params=_params("parallel"),
        name="moe_combine",
    )(gates_t, x1s, out4, p2, w_ple_gate.astype(BF16), w_ple_proj.astype(BF16),
      ln_g.reshape(1, D).astype(F32), ln_b.reshape(1, D).astype(F32))


def kernel(x, p, w_in, lambda_q1, lambda_k1, lambda_q2, lambda_k2, subln_g, ssm_a_re, ssm_a_im,
           ssm_log_dt, ssm_b_re, ssm_b_im, ssm_c_re, ssm_c_im, ssm_d, w_glu, ssm_norm_g, w_out,
           ln1_g, ln1_b, w_router, b_router, w_gate_up, b_gate_up, w_down, b_down, w_ple_gate,
           w_ple_proj, ln2_g, ln2_b):
    B, L, D = x.shape
    T = B * L
    assert D == SUBLANES * LANES, "token-slab layout holds one token per (8, 128) tile"
    for i in range(DEPTH):
        lambda_init = 0.8 - 0.6 * math.exp(-0.3 * i)
        q, kt, v, u = _in_proj(x, w_in[i])
        lam_vecs = jnp.stack([lambda_q1[i], lambda_k1[i], lambda_q2[i], lambda_k2[i]]).astype(F32)
        attn = _attention(q, kt, v, lam_vecs, subln_g[i], lambda_init)
        b_fold, c_fold, a_tile = _ssm_fold_params(ssm_a_re[i], ssm_a_im[i], ssm_log_dt[i],
                                                  ssm_b_re[i], ssm_b_im[i], ssm_c_re[i],
                                                  ssm_c_im[i], B)
        ssm = _ssm(u, b_fold, c_fold, a_tile, ssm_d[i], w_glu[i], ssm_norm_g[i])

        x1, idx, gates, counts = _mix_route(
            x.reshape(T, D), attn.reshape(T, -1), ssm.reshape(T, -1), w_out[i], ln1_g[i],
            ln1_b[i], w_router[i], b_router[i])

        order = jnp.argsort(idx.reshape(-1), stable=True).astype(jnp.int32)
        order = jnp.concatenate([order, jnp.zeros((IDX_WIN,), jnp.int32)])
        gsrc = lax.rem(order, T) * SUBLANES
        sdst = order * SUBLANES
        cnt = counts[:, 0].astype(jnp.int32)
        start = jnp.cumsum(cnt) - cnt
        n_blk_e = (cnt + MOE_ROWS - 1) // MOE_ROWS
        blk_end = jnp.cumsum(n_blk_e)
        n_blocks = (T * TOP_K) // MOE_ROWS + N_EXPERTS
        blk = jnp.arange(n_blocks, dtype=jnp.int32)
        block_e = jnp.minimum(jnp.sum(blk_end[None, :] <= blk[:, None], axis=1),
                              N_EXPERTS - 1).astype(jnp.int32)
        local = blk - (blk_end - n_blk_e)[block_e]
        blk_src = (start[block_e] + local * MOE_ROWS).astype(jnp.int32)
        blk_cnt = jnp.clip(cnt[block_e] - local * MOE_ROWS, 0, MOE_ROWS).astype(jnp.int32)
        n_used = blk_end[-1:].astype(jnp.int32)
        blk_src = jnp.where(blk < n_used, blk_src, 0)
        blk_cnt = jnp.where(blk < n_used, blk_cnt, 0)

        out4 = _experts(x1, gsrc, sdst, block_e, n_used, blk_src, blk_cnt, w_gate_up[i],
                        b_gate_up[i], w_down[i], b_down[i])
        x = _combine(gates.T, x1, out4.reshape(TOP_K, T * SUBLANES, LANES), p[i].reshape(T, -1), w_ple_gate[i], w_ple_proj[i],
                     ln2_g[i], ln2_b[i]).reshape(B, L, D)
    return x
```
